```python
import math
import jax, jax.numpy as jnp
from jax import lax
import numpy as np

D_MODEL = 1024
BATCH = 4
SEQ = 4096
DEPTH = 4

N_A_LAYERS = DEPTH // 2
N_B_LAYERS = DEPTH - N_A_LAYERS
A_HEADS = 16
A_HEAD_DIM = D_MODEL // A_HEADS
MOBA_BLOCK = 256
MOBA_TOPK = 3
MOBA_QCHUNK = 64
B_HEADS = 8
B_HEAD_DIM = D_MODEL // (2 * B_HEADS)
B_QBLOCK = 128
D_FF = 128 * (-(-8 * D_MODEL // (3 * 128)))
CONV_WIDTH = 3
ROPE_THETA = 500000.0
ROPE_FRACTION = 4
LN_EPS = 1e-5
SUBLN_EPS = 1e-5
DEEPNORM_ALPHA = (2 * DEPTH) ** 0.25
DEEPNORM_BETA = (8 * DEPTH) ** -0.25

kernel_name = "yoco_moba_diffattn_convffn_deepnorm"


def partial_rotary(x, positions):
    rot = x.shape[-1] // ROPE_FRACTION
    half = rot // 2
    inv_freq = 1.0 / (ROPE_THETA ** (jnp.arange(half, dtype=jnp.float32) * 2.0 / rot))
    ang = positions.astype(jnp.float32)[:, :, None] * inv_freq
    cos = jnp.cos(ang)[:, :, None, :].astype(x.dtype)
    sin = jnp.sin(ang)[:, :, None, :].astype(x.dtype)
    x1, x2, xp = x[..., :half], x[..., half:rot], x[..., rot:]
    return jnp.concatenate([x1 * cos - x2 * sin, x2 * cos + x1 * sin, xp], axis=-1)


def layer_norm(x, g, b):
    xf = x.astype(jnp.float32)
    mu = jnp.mean(xf, axis=-1, keepdims=True)
    var = jnp.mean(jnp.square(xf - mu), axis=-1, keepdims=True)
    return ((xf - mu) * lax.rsqrt(var + LN_EPS) * g + b).astype(x.dtype)


def conv_ffn(x, w_in, conv_w, conv_b, w_out):
    up, gate = jnp.split(x @ w_in, 2, axis=-1)
    up = lax.conv_general_dilated(
        up, conv_w[:, None, :], window_strides=(1,),
        padding=[(CONV_WIDTH - 1, 0)],
        dimension_numbers=("NWC", "WIO", "NWC"),
        feature_group_count=up.shape[-1]) + conv_b
    return (jax.nn.gelu(up) * gate) @ w_out


def moba_attention(q, k, v):
    B, H, S, Dh = q.shape
    nb = -(-S // MOBA_BLOCK)
    pad = nb * MOBA_BLOCK - S
    kb = jnp.pad(k, ((0, 0), (0, 0), (0, pad), (0, 0))).reshape(B, H, nb, MOBA_BLOCK, Dh)
    vb = jnp.pad(v, ((0, 0), (0, 0), (0, pad), (0, 0))).reshape(B, H, nb, MOBA_BLOCK, Dh)
    scale = Dh ** -0.5
    k_mean = jnp.mean(kb, axis=3)
    gate = jnp.einsum("bhsd,bhnd->bhsn", q, k_mean).astype(jnp.float32)
    q_block = jnp.arange(S) // MOBA_BLOCK
    past = jnp.arange(nb)[None, :] < q_block[:, None]
    gate = jnp.where(past, gate, -jnp.inf)
    topk = min(MOBA_TOPK, nb)
    _, sel = lax.top_k(gate, topk)
    sel_valid = sel < q_block[:, None]
    gather = jax.vmap(jax.vmap(lambda blocks, idx: blocks[idx]))

    def chunk(c):
        start = c * MOBA_QCHUNK
        qc = lax.dynamic_slice_in_dim(q, start, MOBA_QCHUNK, axis=2)
        selc = lax.dynamic_slice_in_dim(sel, start, MOBA_QCHUNK, axis=2)
        validc = lax.dynamic_slice_in_dim(sel_valid, start, MOBA_QCHUNK, axis=2)
        k_sel = gather(kb, selc)
        v_sel = gather(vb, selc)
        own = start // MOBA_BLOCK
        k_own = lax.dynamic_index_in_dim(kb, own, axis=2, keepdims=False)
        v_own = lax.dynamic_index_in_dim(vb, own, axis=2, keepdims=False)
        s_sel = jnp.einsum("bhqd,bhqjld->bhqjl", qc, k_sel).astype(jnp.float32) * scale
        s_sel = jnp.where(validc[..., None], s_sel, -jnp.inf).reshape(B, H, MOBA_QCHUNK, topk * MOBA_BLOCK)
        q_pos = start + jnp.arange(MOBA_QCHUNK)
        k_pos = own * MOBA_BLOCK + jnp.arange(MOBA_BLOCK)
        s_own = jnp.einsum("bhqd,bhld->bhql", qc, k_own).astype(jnp.float32) * scale
        s_own = jnp.where(k_pos[None, :] <= q_pos[:, None], s_own, -jnp.inf)
        p = jax.nn.softmax(jnp.concatenate([s_sel, s_own], axis=-1), axis=-1)
        p_sel = p[..., :topk * MOBA_BLOCK].reshape(B, H, MOBA_QCHUNK, topk, MOBA_BLOCK).astype(v.dtype)
        p_own = p[..., topk * MOBA_BLOCK:].astype(v.dtype)
        return (jnp.einsum("bhqjl,bhqjld->bhqd", p_sel, v_sel)
                + jnp.einsum("bhql,bhld->bhqd", p_own, v_own))

    outs = lax.map(chunk, jnp.arange(S // MOBA_QCHUNK))
    return outs.transpose(1, 2, 0, 3, 4).reshape(B, H, S, Dh)


def diff_attention(q1, q2, k1, k2, v, lam):
    S = q1.shape[2]
    scale = q1.shape[-1] ** -0.5
    k_pos = jnp.arange(S)

    def block(i):
        start = i * B_QBLOCK
        q1b = lax.dynamic_slice_in_dim(q1, start, B_QBLOCK, axis=2)
        q2b = lax.dynamic_slice_in_dim(q2, start, B_QBLOCK, axis=2)
        mask = k_pos[None, :] <= (start + jnp.arange(B_QBLOCK))[:, None]

        def attn_map(qb, kk):
            s = jnp.einsum("bhqd,bhkd->bhqk", qb, kk).astype(jnp.float32) * scale
            return jax.nn.softmax(jnp.where(mask, s, -jnp.inf), axis=-1)

        a = attn_map(q1b, k1) - lam * attn_map(q2b, k2)
        return jnp.einsum("bhqk,bhkd->bhqd", a.astype(v.dtype), v)

    outs = lax.map(block, jnp.arange(S // B_QBLOCK))
    B, H, _, Dv = v.shape
    return outs.transpose(1, 2, 0, 3, 4).reshape(B, H, S, Dv)


def setup_inputs(seed: int = 0) -> dict:
    key = jax.random.key(seed)
    ks = jax.random.split(key, 16)
    D = D_MODEL
    f32 = jnp.float32
    nrm = lambda k, shape, s: jax.random.normal(k, shape, f32) * s
    x = jax.random.normal(ks[0], (BATCH, SEQ, D), f32)
    offsets = jax.random.randint(ks[1], (BATCH, 1), 0, 1024, dtype=jnp.int32)
    positions = offsets + jnp.arange(SEQ, dtype=jnp.int32)[None, :]
    qkv_scale = jnp.concatenate([jnp.ones((2 * D,), f32), jnp.full((D,), DEEPNORM_BETA, f32)])
    a_w_qkv = nrm(ks[2], (N_A_LAYERS, D, 3 * D), D ** -0.5) * qkv_scale
    a_w_o = nrm(ks[3], (N_A_LAYERS, D, D), D ** -0.5 * DEEPNORM_BETA)
    kv_scale = jnp.concatenate([jnp.ones((D,), f32), jnp.full((D,), DEEPNORM_BETA, f32)])
    w_kv_shared = nrm(ks[4], (D, 2 * D), D ** -0.5) * kv_scale
    b_w_q = nrm(ks[5], (N_B_LAYERS, D, D), D ** -0.5)
    b_w_o = nrm(ks[6], (N_B_LAYERS, D, D), D ** -0.5 * DEEPNORM_BETA)
    b_lambda = nrm(ks[7], (N_B_LAYERS, 4, B_HEAD_DIM), 0.1)
    b_subln_g = 1.0 + nrm(ks[8], (N_B_LAYERS, 2 * B_HEAD_DIM), 0.02)
    ln_g = 1.0 + nrm(ks[9], (DEPTH, 2, D), 0.02)
    ln_b = nrm(ks[10], (DEPTH, 2, D), 0.02)
    ffn_w_in = nrm(ks[11], (DEPTH, D, 2 * D_FF), D ** -0.5)
    ffn_conv_w = nrm(ks[12], (DEPTH, CONV_WIDTH, D_FF), CONV_WIDTH ** -0.5)
    ffn_conv_b = nrm(ks[13], (DEPTH, D_FF), 0.01)
    ffn_w_out = nrm(ks[14], (DEPTH, D_FF, D), D_FF ** -0.5 * DEEPNORM_BETA)
    return {"x": x, "positions": positions, "a_w_qkv": a_w_qkv, "a_w_o": a_w_o,
            "w_kv_shared": w_kv_shared, "b_w_q": b_w_q, "b_w_o": b_w_o,
            "b_lambda": b_lambda, "b_subln_g": b_subln_g, "ln_g": ln_g, "ln_b": ln_b,
            "ffn_w_in": ffn_w_in, "ffn_conv_w": ffn_conv_w, "ffn_conv_b": ffn_conv_b,
            "ffn_w_out": ffn_w_out}


def reference(x, positions, a_w_qkv, a_w_o, w_kv_shared, b_w_q, b_w_o, b_lambda,
              b_subln_g, ln_g, ln_b, ffn_w_in, ffn_conv_w, ffn_conv_b, ffn_w_out):
    B, S, D = x.shape
    k1 = k2 = v_sh = None
    for layer in range(DEPTH):
        if layer < N_A_LAYERS:
            qkv = (x @ a_w_qkv[layer]).reshape(B, S, 3, A_HEADS, A_HEAD_DIM)
            q = partial_rotary(qkv[:, :, 0], positions).transpose(0, 2, 1, 3)
            k = partial_rotary(qkv[:, :, 1], positions).transpose(0, 2, 1, 3)
            v = qkv[:, :, 2].transpose(0, 2, 1, 3)
            o = moba_attention(q, k, v).transpose(0, 2, 1, 3).reshape(B, S, D)
            x = layer_norm(DEEPNORM_ALPHA * x + o @ a_w_o[layer], ln_g[layer, 0], ln_b[layer, 0])
        else:
            j = layer - N_A_LAYERS
            if j == 0:
                kv = x @ w_kv_shared
                kk = partial_rotary(kv[..., :D].reshape(B, S, 2 * B_HEADS, B_HEAD_DIM), positions)
                kk = kk.reshape(B, S, B_HEADS, 2, B_HEAD_DIM)
                k1 = kk[:, :, :, 0].transpose(0, 2, 1, 3)
                k2 = kk[:, :, :, 1].transpose(0, 2, 1, 3)
                v_sh = kv[..., D:].reshape(B, S, B_HEADS, 2 * B_HEAD_DIM).transpose(0, 2, 1, 3)
            lambda_init = 0.8 - 0.6 * math.exp(-0.3 * layer)
            lp = b_lambda[j].astype(jnp.float32)
            lam = jnp.exp(jnp.sum(lp[0] * lp[1])) - jnp.exp(jnp.sum(lp[2] * lp[3])) + lambda_init
            qq = partial_rotary((x @ b_w_q[j]).reshape(B, S, 2 * B_HEADS, B_HEAD_DIM), positions)
            qq = qq.reshape(B, S, B_HEADS, 2, B_HEAD_DIM)
            q1 = qq[:, :, :, 0].transpose(0, 2, 1, 3)
            q2 = qq[:, :, :, 1].transpose(0, 2, 1, 3)
            o = diff_attention(q1, q2, k1, k2, v_sh, lam).astype(jnp.float32)
            o = o * lax.rsqrt(jnp.mean(jnp.square(o), axis=-1, keepdims=True) + SUBLN_EPS)
            o = (o * b_subln_g[j] * (1.0 - lambda_init)).astype(x.dtype)
            o = o.transpose(0, 2, 1, 3).reshape(B, S, D)
            x = layer_norm(DEEPNORM_ALPHA * x + o @ b_w_o[j], ln_g[layer, 0], ln_b[layer, 0])
        f = conv_ffn(x, ffn_w_in[layer], ffn_conv_w[layer], ffn_conv_b[layer], ffn_w_out[layer])
        x = layer_norm(DEEPNORM_ALPHA * x + f, ln_g[layer, 1], ln_b[layer, 1])
    return x
```

```python
import functools
import math

import jax
import jax.numpy as jnp
from jax import lax
from jax.experimental import pallas as pl
from jax.experimental.pallas import tpu as pltpu

F32 = jnp.float32
BF16 = jnp.bfloat16

HEAD_DIM = 64
LANES = 128
MOBA_BLOCK = 256
MOBA_TOPK = 3
ROPE_THETA = 500000.0
ROPE_ROT = HEAD_DIM // 4
ROPE_HALF = ROPE_ROT // 2
LN_EPS = 1e-5
SUBLN_EPS = 1e-5
CONV_WIDTH = 3
NEG = -1e30

VMEM_LIMIT = 56 * 1024 * 1024
PROJ_TM = 512
FFN_TM = 512
FFN_HALO = 16
ATT_TQ = 256
ATT_TK = 256

_NT = (((1,), (1,)), ((), ()))


def _cparams(sem):
    return pltpu.CompilerParams(dimension_semantics=sem, vmem_limit_bytes=VMEM_LIMIT)


def _proj_rot_kernel(x_ref, w_ref, c_ref, s_ref, o_ref, *, n_rot_tiles):
    xb = x_ref[...].astype(BF16)
    y = jnp.dot(xb, w_ref[...], preferred_element_type=F32)
    tm, tn = y.shape
    c = c_ref[...]
    s = s_ref[...]
    lane = lax.broadcasted_iota(jnp.int32, (tm, LANES), 1)
    first_half = (lane & (HEAD_DIM - 1)) < ROPE_HALF
    cols = []
    for g in range(tn // LANES):
        yg = y[:, g * LANES:(g + 1) * LANES]
        partner = jnp.where(first_half, pltpu.roll(yg, LANES - ROPE_HALF, 1), pltpu.roll(yg, ROPE_HALF, 1))
        cols.append(yg * c + partner * s)
    yr = jnp.concatenate(cols, axis=1)
    use_rot = pl.program_id(1) < n_rot_tiles
    o_ref[...] = jnp.where(use_rot, yr, y).astype(o_ref.dtype)


def _proj_rot(x2d, w_bf16, cos_tab, sin_tab, n_rot_cols, tn):
    t, k = x2d.shape
    n = w_bf16.shape[1]
    tm = PROJ_TM
    return pl.pallas_call(
        functools.partial(_proj_rot_kernel, n_rot_tiles=n_rot_cols // tn),
        grid=(t // tm, n // tn),
        in_specs=[
            pl.BlockSpec((tm, k), lambda i, j: (i, 0)),
            pl.BlockSpec((k, tn), lambda i, j: (0, j)),
            pl.BlockSpec((tm, LANES), lambda i, j: (i, 0)),
            pl.BlockSpec((tm, LANES), lambda i, j: (i, 0)),
        ],
        out_specs=pl.BlockSpec((tm, tn), lambda i, j: (i, j)),
        out_shape=jax.ShapeDtypeStruct((t, n), BF16),
        compiler_params=_cparams(("parallel", "arbitrary")),
        name="proj_rot",
    )(x2d, w_bf16, cos_tab, sin_tab)


def _split_heads(q2):
    lane = lax.broadcasted_iota(jnp.int32, q2.shape, 1)
    zero = jnp.zeros_like(q2)
    q_a = jnp.where(lane < HEAD_DIM, q2, zero)
    q_b = jnp.where(lane >= HEAD_DIM, q2, zero)
    return jnp.concatenate([q_a, q_b], axis=0)


def _softmax_first(s, v_blk):
    m = jnp.max(s, axis=1, keepdims=True)
    p = jnp.exp(s - m)
    l = jnp.sum(p, axis=1, keepdims=True)
    acc = jnp.dot(p.astype(BF16), v_blk, preferred_element_type=F32)
    return m, l, acc


def _softmax_update(s, v_blk, m_ref, l_ref, acc_ref):
    m_old = m_ref[...]
    m_new = jnp.maximum(m_old, jnp.max(s, axis=1, keepdims=True))
    alpha = jnp.exp(m_old - m_new)
    p = jnp.exp(s - m_new)
    l_ref[...] = alpha * l_ref[...] + jnp.sum(p, axis=1, keepdims=True)
    acc_ref[...] = alpha * acc_ref[...] + jnp.dot(p.astype(BF16), v_blk, preferred_element_type=F32)
    m_ref[...] = m_new


def _causal_mask(s, tq):
    rows, cols = s.shape
    r = lax.broadcasted_iota(jnp.int32, (rows, cols), 0) & (tq - 1)
    c = lax.broadcasted_iota(jnp.int32, (rows, cols), 1)
    return jnp.where(c <= r, s, NEG)


def _moba_kernel(q_ref, k_ref, v_ref, o_ref, kmean_sc, m_sc, l_sc, acc_sc, *, n_blocks):
    i = pl.program_id(2)
    tq = ATT_TQ
    scale = HEAD_DIM ** -0.5

    @pl.when(i == 0)
    def _():
        for j in range(n_blocks):
            kj = k_ref[0, j * MOBA_BLOCK:(j + 1) * MOBA_BLOCK, :].astype(F32)
            kmean_sc[j:j + 1, :] = jnp.mean(kj, axis=0, keepdims=True)

    qs = _split_heads(q_ref[0])

    gate_t = lax.dot_general(kmean_sc[...].astype(BF16), qs, _NT, preferred_element_type=F32)
    blk = lax.broadcasted_iota(jnp.int32, gate_t.shape, 0)
    cnt = jnp.zeros(gate_t.shape, F32)
    for jp in range(n_blocks):
        row = gate_t[jp:jp + 1, :]
        beats = (row > gate_t) | ((row == gate_t) & (jp < blk))
        cnt = cnt + jnp.where(beats & (jp < i), 1.0, 0.0)
    selected = (cnt < MOBA_TOPK) & (blk < i)
    bias_t = jnp.where(selected, 0.0, NEG)
    bias_pad = jnp.concatenate([bias_t, jnp.zeros((LANES - n_blocks, 2 * tq), F32)], axis=0)
    bias = bias_pad.T.astype(BF16)

    qsc = qs * jnp.asarray(scale, BF16)
    q_aug = jnp.concatenate([qsc, bias], axis=1)

    own = pl.multiple_of(i * MOBA_BLOCK, MOBA_BLOCK)
    s = lax.dot_general(qsc, k_ref[0, pl.ds(own, MOBA_BLOCK), :], _NT, preferred_element_type=F32)
    m, l, acc = _softmax_first(_causal_mask(s, tq), v_ref[0, pl.ds(own, MOBA_BLOCK), :])
    m_sc[...] = m
    l_sc[...] = l
    acc_sc[...] = acc

    blk_lane = lax.broadcasted_iota(jnp.int32, (MOBA_BLOCK, LANES), 1)

    def body(j, carry):
        start = pl.multiple_of(j * MOBA_BLOCK, MOBA_BLOCK)
        k_j = k_ref[0, pl.ds(start, MOBA_BLOCK), :]
        onehot = jnp.where(blk_lane == j, 1.0, 0.0).astype(BF16)
        k_aug = jnp.concatenate([k_j, onehot], axis=1)
        s_j = lax.dot_general(q_aug, k_aug, _NT, preferred_element_type=F32)
        _softmax_update(s_j, v_ref[0, pl.ds(start, MOBA_BLOCK), :], m_sc, l_sc, acc_sc)
        return carry

    lax.fori_loop(0, i, body, 0)

    o = acc_sc[...] / l_sc[...]
    lane = lax.broadcasted_iota(jnp.int32, (tq, LANES), 1)
    o_ref[0] = jnp.where(lane < HEAD_DIM, o[:tq], o[tq:]).astype(o_ref.dtype)


def _moba_attention(qkv, d_model):
    b, s, _ = qkv.shape
    n_groups = d_model // LANES
    n_blocks = s // MOBA_BLOCK
    return pl.pallas_call(
        functools.partial(_moba_kernel, n_blocks=n_blocks),
        grid=(b, n_groups, s // ATT_TQ),
        in_specs=[
            pl.BlockSpec((1, ATT_TQ, LANES), lambda bi, g, i: (bi, i, g)),
            pl.BlockSpec((1, s, LANES), lambda bi, g, i: (bi, 0, n_groups + g)),
            pl.BlockSpec((1, s, LANES), lambda bi, g, i: (bi, 0, 2 * n_groups + g)),
        ],
        out_specs=pl.BlockSpec((1, ATT_TQ, LANES), lambda bi, g, i: (bi, i, g)),
        out_shape=jax.ShapeDtypeStruct((b, s, d_model), BF16),
        scratch_shapes=[
            pltpu.VMEM((n_blocks, LANES), F32),
            pltpu.VMEM((2 * ATT_TQ, 1), F32),
            pltpu.VMEM((2 * ATT_TQ, 1), F32),
            pltpu.VMEM((2 * ATT_TQ, LANES), F32),
        ],
        compiler_params=_cparams(("parallel", "parallel", "arbitrary")),
        name="moba_attn",
    )(qkv, qkv, qkv)


def _diff_kernel(q_ref, k_ref, v_ref, lam_ref, g_ref, o_ref, m_sc, l_sc, acc_sc, *, lambda_init):
    i = pl.program_id(2)
    tq = ATT_TQ
    scale = HEAD_DIM ** -0.5
    qsc = _split_heads(q_ref[0]) * jnp.asarray(scale, BF16)

    own = pl.multiple_of(i * ATT_TK, ATT_TK)
    s = lax.dot_general(qsc, k_ref[0, pl.ds(own, ATT_TK), :], _NT, preferred_element_type=F32)
    m, l, acc = _softmax_first(_causal_mask(s, tq), v_ref[0, pl.ds(own, ATT_TK), :])
    m_sc[...] = m
    l_sc[...] = l
    acc_sc[...] = acc

    def body(j, carry):
        start = pl.multiple_of(j * ATT_TK, ATT_TK)
        s_j = lax.dot_general(qsc, k_ref[0, pl.ds(start, ATT_TK), :], _NT, preferred_element_type=F32)
        _softmax_update(s_j, v_ref[0, pl.ds(start, ATT_TK), :], m_sc, l_sc, acc_sc)
        return carry

    lax.fori_loop(0, i, body, 0)

    lp = lam_ref[...].astype(F32)
    lam = (jnp.exp(jnp.sum(lp[0:1] * lp[1:2], axis=1, keepdims=True))
           - jnp.exp(jnp.sum(lp[2:3] * lp[3:4], axis=1, keepdims=True)) + lambda_init)
    o_all = acc_sc[...] / l_sc[...]
    o = o_all[:tq] - lam * o_all[tq:]
    o = o * lax.rsqrt(jnp.mean(o * o, axis=1, keepdims=True) + SUBLN_EPS)
    o_ref[0] = (o * g_ref[...] * (1.0 - lambda_init)).astype(o_ref.dtype)


def _diff_attention(q, kv, lam_params, subln_g, lambda_init, d_model):
    b, s, _ = q.shape
    n_heads = d_model // LANES
    return pl.pallas_call(
        functools.partial(_diff_kernel, lambda_init=lambda_init),
        grid=(b, n_heads, s // ATT_TQ),
        in_specs=[
            pl.BlockSpec((1, ATT_TQ, LANES), lambda bi, h, i: (bi, i, h)),
            pl.BlockSpec((1, s, LANES), lambda bi, h, i: (bi, 0, h)),
            pl.BlockSpec((1, s, LANES), lambda bi, h, i: (bi, 0, n_heads + h)),
            pl.BlockSpec(lam_params.shape, lambda bi, h, i: (0, 0)),
            pl.BlockSpec((1, LANES), lambda bi, h, i: (0, 0)),
        ],
        out_specs=pl.BlockSpec((1, ATT_TQ, LANES), lambda bi, h, i: (bi, i, h)),
        out_shape=jax.ShapeDtypeStruct((b, s, d_model), BF16),
        scratch_shapes=[
            pltpu.VMEM((2 * ATT_TQ, 1), F32),
            pltpu.VMEM((2 * ATT_TQ, 1), F32),
            pltpu.VMEM((2 * ATT_TQ, LANES), F32),
        ],
        compiler_params=_cparams(("parallel", "parallel", "arbitrary")),
        name="diff_attn",
    )(q, kv, kv, lam_params, subln_g)


def _layer_norm(z, g, b):
    mu = jnp.mean(z, axis=1, keepdims=True)
    zc = z - mu
    var = jnp.mean(zc * zc, axis=1, keepdims=True)
    return zc * lax.rsqrt(var + LN_EPS) * g + b


def _oproj_ln_kernel(x_ref, a_ref, w_ref, g_ref, b_ref, o_ref, *, alpha):
    y = jnp.dot(a_ref[...], w_ref[...], preferred_element_type=F32)
    o_ref[...] = _layer_norm(alpha * x_ref[...] + y, g_ref[...], b_ref[...])


def _oproj_ln(x2d, a2d, w_bf16, g, b, alpha):
    t, d = x2d.shape
    tm = PROJ_TM
    return pl.pallas_call(
        functools.partial(_oproj_ln_kernel, alpha=alpha),
        grid=(t // tm,),
        in_specs=[
            pl.BlockSpec((tm, d), lambda i: (i, 0)),
            pl.BlockSpec((tm, d), lambda i: (i, 0)),
            pl.BlockSpec((d, d), lambda i: (0, 0)),
            pl.BlockSpec((1, d), lambda i: (0, 0)),
            pl.BlockSpec((1, d), lambda i: (0, 0)),
        ],
        out_specs=pl.BlockSpec((tm, d), lambda i: (i, 0)),
        out_shape=jax.ShapeDtypeStruct((t, d), F32),
        compiler_params=_cparams(("parallel",)),
        name="oproj_ln",
    )(x2d, a2d, w_bf16, g, b)


def _gelu_tanh(x):
    return 0.5 * x * (1.0 + jnp.tanh(math.sqrt(2.0 / math.pi) * (x + 0.044715 * (x * x * x))))


def _ffn_kernel(x_ref, halo_ref, wu_ref, wg_ref, cw_ref, cb_ref, wo_ref, g_ref, b_ref, o_ref, acc_sc,
                *, alpha, tiles_per_seq):
    i = pl.program_id(0)
    j = pl.program_id(1)
    tm = x_ref.shape[0]
    x = x_ref[...]
    xb = x.astype(BF16)
    seq_start = (i % tiles_per_seq) == 0
    halo = jnp.where(seq_start, 0.0, halo_ref[...]).astype(BF16)
    x_ext = jnp.concatenate([halo, xb], axis=0)
    up = jnp.dot(x_ext, wu_ref[...], preferred_element_type=F32)
    gate = jnp.dot(xb, wg_ref[...], preferred_element_type=F32)
    cw = cw_ref[...]
    conv = (up[FFN_HALO - 2:FFN_HALO - 2 + tm] * cw[0:1]
            + up[FFN_HALO - 1:FFN_HALO - 1 + tm] * cw[1:2]
            + up[FFN_HALO:FFN_HALO + tm] * cw[2:3]
            + cb_ref[...])
    h = (_gelu_tanh(conv) * gate).astype(BF16)
    part = jnp.dot(h, wo_ref[...], preferred_element_type=F32)

    @pl.when(j == 0)
    def _():
        acc_sc[...] = part

    @pl.when(j > 0)
    def _():
        acc_sc[...] += part

    @pl.when(j == pl.num_programs(1) - 1)
    def _():
        o_ref[...] = _layer_norm(alpha * x + acc_sc[...], g_ref[...], b_ref[...])


def _ffn_tile(d_ff):
    n = d_ff // LANES
    for parts in range(2, n + 1):
        if n % parts == 0:
            return d_ff // parts
    return d_ff


def _conv_ffn_ln(x2d, w_in_bf16, conv_w, conv_b, w_out_bf16, g, b, alpha, seq):
    t, d = x2d.shape
    d_ff = w_out_bf16.shape[0]
    tm = FFN_TM
    tf = _ffn_tile(d_ff)
    nf = d_ff // tf
    halo_blocks = tm // FFN_HALO
    return pl.pallas_call(
        functools.partial(_ffn_kernel, alpha=alpha, tiles_per_seq=seq // tm),
        grid=(t // tm, nf),
        in_specs=[
            pl.BlockSpec((tm, d), lambda i, j: (i, 0)),
            pl.BlockSpec((FFN_HALO, d), lambda i, j: (jnp.maximum(i * halo_blocks - 1, 0), 0)),
            pl.BlockSpec((d, tf), lambda i, j: (0, j)),
            pl.BlockSpec((d, tf), lambda i, j: (0, nf + j)),
            pl.BlockSpec((CONV_WIDTH, tf), lambda i, j: (0, j)),
            pl.BlockSpec((1, tf), lambda i, j: (0, j)),
            pl.BlockSpec((tf, d), lambda i, j: (j, 0)),
            pl.BlockSpec((1, d), lambda i, j: (0, 0)),
            pl.BlockSpec((1, d), lambda i, j: (0, 0)),
        ],
        out_specs=pl.BlockSpec((tm, d), lambda i, j: (i, 0)),
        out_shape=jax.ShapeDtypeStruct((t, d), F32),
        scratch_shapes=[pltpu.VMEM((tm, d), F32)],
        compiler_params=_cparams(("parallel", "arbitrary")),
        name="conv_ffn_ln",
    )(x2d, x2d, w_in_bf16, w_in_bf16, conv_w, conv_b, w_out_bf16, g, b)


def _rotary_tables(positions):
    inv_freq = 1.0 / (ROPE_THETA ** (jnp.arange(ROPE_HALF, dtype=F32) * 2.0 / ROPE_ROT))
    ang = positions.astype(F32).reshape(-1, 1) * inv_freq
    cos, sin = jnp.cos(ang), jnp.sin(ang)
    t = ang.shape[0]
    ones = jnp.ones((t, HEAD_DIM - ROPE_ROT), F32)
    zeros = jnp.zeros((t, HEAD_DIM - ROPE_ROT), F32)
    c_head = jnp.concatenate([cos, cos, ones], axis=1)
    s_head = jnp.concatenate([-sin, sin, zeros], axis=1)
    reps = LANES // HEAD_DIM
    return jnp.tile(c_head, (1, reps)), jnp.tile(s_head, (1, reps))


def kernel(x, positions, a_w_qkv, a_w_o, w_kv_shared, b_w_q, b_w_o, b_lambda, b_subln_g, ln_g, ln_b,
           ffn_w_in, ffn_conv_w, ffn_conv_b, ffn_w_out):
    bsz, seq, d = x.shape
    depth = ffn_w_in.shape[0]
    n_a = a_w_qkv.shape[0]
    t = bsz * seq
    alpha = (2 * depth) ** 0.25
    assert d % LANES == 0 and seq % max(PROJ_TM, FFN_TM, MOBA_BLOCK) == 0

    cos_tab, sin_tab = _rotary_tables(positions)
    xs = x.reshape(t, d)
    kv = None
    for layer in range(depth):
        g0, b0 = ln_g[layer, 0].reshape(1, d), ln_b[layer, 0].reshape(1, d)
        g1, b1 = ln_g[layer, 1].reshape(1, d), ln_b[layer, 1].reshape(1, d)
        if layer < n_a:
            qkv = _proj_rot(xs, a_w_qkv[layer].astype(BF16), cos_tab, sin_tab, n_rot_cols=2 * d, tn=d)
            o = _moba_attention(qkv.reshape(bsz, seq, 3 * d), d)
            xs = _oproj_ln(xs, o.reshape(t, d), a_w_o[layer].astype(BF16), g0, b0, alpha)
        else:
            jb = layer - n_a
            if jb == 0:
                kv = _proj_rot(xs, w_kv_shared.astype(BF16), cos_tab, sin_tab, n_rot_cols=d, tn=d)
                kv = kv.reshape(bsz, seq, 2 * d)
            lambda_init = 0.8 - 0.6 * math.exp(-0.3 * layer)
            q = _proj_rot(xs, b_w_q[jb].astype(BF16), cos_tab, sin_tab, n_rot_cols=d, tn=d)
            o = _diff_attention(q.reshape(bsz, seq, d), kv, b_lambda[jb], b_subln_g[jb].reshape(1, LANES),
                                lambda_init, d)
            xs = _oproj_ln(xs, o.reshape(t, d), b_w_o[jb].astype(BF16), g0, b0, alpha)
        xs = _conv_ffn_ln(xs, ffn_w_in[layer].astype(BF16), ffn_conv_w[layer], ffn_conv_b[layer].reshape(1, -1),
                          ffn_w_out[layer].astype(BF16), g1, b1, alpha, seq)
    return xs.reshape(bsz, seq, d)
```

```python
import functools
import math

import jax
import jax.numpy as jnp
from jax import lax
from jax.experimental import pallas as pl
from jax.experimental.pallas import tpu as pltpu

F32 = jnp.float32
BF16 = jnp.bfloat16

HEAD_DIM = 64
LANES = 128
MOBA_BLOCK = 256
MOBA_TOPK = 3
ROPE_THETA = 500000.0
ROPE_ROT = HEAD_DIM // 4
ROPE_HALF = ROPE_ROT // 2
LN_EPS = 1e-5
SUBLN_EPS = 1e-5
CONV_WIDTH = 3
NEG = -1e30

VMEM_LIMIT = 56 * 1024 * 1024
PROJ_TM = 512
FFN_TM = 512
FFN_HALO = 16
ATT_BLK = MOBA_BLOCK
ATT_CHUNK = 4
ATT_GROUPS = 2
Q_SCALE = HEAD_DIM ** -0.5 * math.log2(math.e)

_NT = (((1,), (1,)), ((), ()))


def _cparams(sem):
    return pltpu.CompilerParams(dimension_semantics=sem, vmem_limit_bytes=VMEM_LIMIT)


def _proj_rot_kernel(x_ref, w_ref, c_ref, s_ref, o_ref, *, n_scaled_tiles):
    xb = x_ref[...].astype(BF16)
    y = jnp.dot(xb, w_ref[...], preferred_element_type=F32)
    y = y * jnp.where(pl.program_id(1) < n_scaled_tiles, Q_SCALE, 1.0)
    tm, tn = y.shape
    c = c_ref[...]
    s = s_ref[...]
    lane = lax.broadcasted_iota(jnp.int32, (tm, LANES), 1)
    first_half = (lane & (HEAD_DIM - 1)) < ROPE_HALF
    cols = []
    for g in range(tn // LANES):
        yg = y[:, g * LANES:(g + 1) * LANES]
        partner = jnp.where(first_half, pltpu.roll(yg, LANES - ROPE_HALF, 1), pltpu.roll(yg, ROPE_HALF, 1))
        cols.append(yg * c + partner * s)
    o_ref[...] = jnp.concatenate(cols, axis=1).astype(o_ref.dtype)


def _proj_rot(x2d, w_bf16, cos_tab, sin_tab, tn, n_scaled_tiles):
    t, k = x2d.shape
    n = w_bf16.shape[1]
    tm = PROJ_TM
    return pl.pallas_call(
        functools.partial(_proj_rot_kernel, n_scaled_tiles=n_scaled_tiles),
        grid=(t // tm, n // tn),
        in_specs=[
            pl.BlockSpec((tm, k), lambda i, j: (i, 0)),
            pl.BlockSpec((k, tn), lambda i, j: (0, j)),
            pl.BlockSpec((tm, LANES), lambda i, j: (i, 0)),
            pl.BlockSpec((tm, LANES), lambda i, j: (i, 0)),
        ],
        out_specs=pl.BlockSpec((tm, tn), lambda i, j: (i, j)),
        out_shape=jax.ShapeDtypeStruct((t, n), BF16),
        compiler_params=_cparams(("parallel", "arbitrary")),
        name="proj_rot",
    )(x2d, w_bf16, cos_tab, sin_tab)


def _proj_t_kernel(x_ref, wt_ref, o_ref):
    xb = x_ref[0].astype(BF16)
    o_ref[0, 0] = lax.dot_general(wt_ref[...], xb, _NT, preferred_element_type=F32).astype(o_ref.dtype)


def _proj_t(x3d, wt_bf16):
    b, s, d = x3d.shape
    n = wt_bf16.shape[0]
    return pl.pallas_call(
        _proj_t_kernel,
        grid=(b, s // ATT_BLK),
        in_specs=[
            pl.BlockSpec((1, ATT_BLK, d), lambda bi, j: (bi, j, 0)),
            pl.BlockSpec((n, d), lambda bi, j: (0, 0)),
        ],
        out_specs=pl.BlockSpec((1, 1, n, ATT_BLK), lambda bi, j: (bi, j, 0, 0)),
        out_shape=jax.ShapeDtypeStruct((b, s // ATT_BLK, n, ATT_BLK), BF16),
        compiler_params=_cparams(("parallel", "parallel")),
        name="proj_t",
    )(x3d, wt_bf16)


def _split_heads(q2):
    lane = lax.broadcasted_iota(jnp.int32, q2.shape, 1)
    zero = jnp.zeros_like(q2)
    q_a = jnp.where(lane < HEAD_DIM, q2, zero)
    q_b = jnp.where(lane >= HEAD_DIM, q2, zero)
    return jnp.concatenate([q_a, q_b], axis=0)


def _attend(q_aug, k_ref, vt_ref, e_ref, i, m_sc, l_sc, acc_sc):
    tk = ATT_CHUNK * ATT_BLK
    groups = range(len(q_aug))

    def k_rows(g, start, size):
        return k_ref[0, pl.ds(start, size), g * LANES:(g + 1) * LANES]

    def vt_cols(g, blk0, n):
        return jnp.concatenate([vt_ref[0, blk0 + b, g * LANES:(g + 1) * LANES, :] for b in range(n)], axis=1)

    own = pl.multiple_of(i * ATT_BLK, ATT_BLK)
    for g in groups:
        s_t = lax.dot_general(k_rows(g, own, ATT_BLK), q_aug[g][:, :LANES], _NT, preferred_element_type=F32)
        key_pos = lax.broadcasted_iota(jnp.int32, s_t.shape, 0)
        q_pos = lax.broadcasted_iota(jnp.int32, s_t.shape, 1) & (ATT_BLK - 1)
        s_t = jnp.where(key_pos <= q_pos, s_t, NEG)
        m = jnp.max(s_t, axis=0, keepdims=True)
        p = jnp.exp2(s_t - m)
        m_sc[g] = m
        l_sc[g] = jnp.sum(p, axis=0, keepdims=True)
        acc_sc[g] = jnp.dot(vt_cols(g, i, 1), p.astype(BF16), preferred_element_type=F32)

    def body(c, carry):
        start = pl.multiple_of(c * tk, tk)
        e_c = e_ref[pl.ds(start, tk), :]
        s_c = [lax.dot_general(jnp.concatenate([k_rows(g, start, tk), e_c], axis=1), q_aug[g], _NT,
                               preferred_element_type=F32) for g in groups]
        for g in groups:
            m_old = m_sc[g]
            m_new = jnp.maximum(m_old, jnp.max(s_c[g], axis=0, keepdims=True))
            alpha = jnp.exp2(m_old - m_new)
            p = jnp.exp2(s_c[g] - m_new)
            l_sc[g] = alpha * l_sc[g] + jnp.sum(p, axis=0, keepdims=True)
            pv = jnp.dot(vt_cols(g, c * ATT_CHUNK, ATT_CHUNK), p.astype(BF16), preferred_element_type=F32)
            acc_sc[g] = alpha * acc_sc[g] + pv
            m_sc[g] = m_new
        return carry

    lax.fori_loop(0, (i + ATT_CHUNK - 1) // ATT_CHUNK, body, 0)
    return [acc_sc[g] / l_sc[g] for g in groups]


def _attn_scratch():
    return [
        pltpu.VMEM((ATT_GROUPS, 1, 2 * ATT_BLK), F32),
        pltpu.VMEM((ATT_GROUPS, 1, 2 * ATT_BLK), F32),
        pltpu.VMEM((ATT_GROUPS, LANES, 2 * ATT_BLK), F32),
    ]


def _moba_kernel(q_ref, k_ref, vt_ref, e_ref, o_ref, kmean_sc, m_sc, l_sc, acc_sc, *, n_blocks):
    i = pl.program_id(2)
    tq = ATT_BLK

    @pl.when(i == 0)
    def _():
        for g in range(ATT_GROUPS):
            for j in range(n_blocks):
                kj = k_ref[0, j * MOBA_BLOCK:(j + 1) * MOBA_BLOCK, g * LANES:(g + 1) * LANES].astype(F32)
                kmean_sc[g, j:j + 1, :] = jnp.mean(kj, axis=0, keepdims=True)

    q_aug = []
    for g in range(ATT_GROUPS):
        qs = _split_heads(q_ref[0, :, g * LANES:(g + 1) * LANES])
        gate_t = lax.dot_general(kmean_sc[g].astype(BF16), qs, _NT, preferred_element_type=F32)
        blk = lax.broadcasted_iota(jnp.int32, gate_t.shape, 0)
        cnt = jnp.zeros(gate_t.shape, F32)
        for jp in range(n_blocks):
            row = gate_t[jp:jp + 1, :]
            beats = (row > gate_t) | ((row == gate_t) & (jp < blk))
            cnt = cnt + jnp.where(beats & (jp < i), 1.0, 0.0)
        selected = (cnt < MOBA_TOPK) & (blk < i)
        bias_t = jnp.where(selected, 0.0, NEG)
        bias_pad = jnp.concatenate([bias_t, jnp.full((LANES - n_blocks, 2 * tq), NEG, F32)], axis=0)
        q_aug.append(jnp.concatenate([qs, bias_pad.T.astype(BF16)], axis=1))

    o_t = _attend(q_aug, k_ref, vt_ref, e_ref, i, m_sc, l_sc, acc_sc)
    for g in range(ATT_GROUPS):
        o_sel = jnp.concatenate([o_t[g][:HEAD_DIM, :tq], o_t[g][HEAD_DIM:, tq:]], axis=0)
        o_ref[0, :, g * LANES:(g + 1) * LANES] = o_sel.T.astype(o_ref.dtype)


def _moba_attention(qk, vt, onehot, d_model):
    b, s, _ = qk.shape
    gw = ATT_GROUPS * LANES
    n_steps = d_model // gw
    n_blocks = s // MOBA_BLOCK
    return pl.pallas_call(
        functools.partial(_moba_kernel, n_blocks=n_blocks),
        grid=(b, n_steps, n_blocks),
        in_specs=[
            pl.BlockSpec((1, ATT_BLK, gw), lambda bi, g, i: (bi, i, g)),
            pl.BlockSpec((1, s, gw), lambda bi, g, i: (bi, 0, n_steps + g)),
            pl.BlockSpec((1, n_blocks, gw, ATT_BLK), lambda bi, g, i: (bi, 0, g, 0)),
            pl.BlockSpec((s, LANES), lambda bi, g, i: (0, 0)),
        ],
        out_specs=pl.BlockSpec((1, ATT_BLK, gw), lambda bi, g, i: (bi, i, g)),
        out_shape=jax.ShapeDtypeStruct((b, s, d_model), BF16),
        scratch_shapes=[pltpu.VMEM((ATT_GROUPS, n_blocks, LANES), F32)] + _attn_scratch(),
        compiler_params=_cparams(("parallel", "parallel", "arbitrary")),
        name="moba_attn",
    )(qk, qk, vt, onehot)


def _diff_kernel(q_ref, k_ref, vt_ref, e_ref, lam_ref, g_ref, o_ref, m_sc, l_sc, acc_sc, *, lambda_init):
    i = pl.program_id(2)
    tq = ATT_BLK
    lane = lax.broadcasted_iota(jnp.int32, (2 * tq, LANES), 1)
    bias = jnp.where(lane < i, 0.0, NEG).astype(BF16)
    q_aug = [jnp.concatenate([_split_heads(q_ref[0, :, g * LANES:(g + 1) * LANES]), bias], axis=1)
             for g in range(ATT_GROUPS)]
    o_t = _attend(q_aug, k_ref, vt_ref, e_ref, i, m_sc, l_sc, acc_sc)

    lp = lam_ref[...].astype(F32)
    lam = (jnp.exp(jnp.sum(lp[0:1] * lp[1:2], axis=1, keepdims=True))
           - jnp.exp(jnp.sum(lp[2:3] * lp[3:4], axis=1, keepdims=True)) + lambda_init)
    for g in range(ATT_GROUPS):
        o = (o_t[g][:, :tq] - lam * o_t[g][:, tq:]).T
        o = o * lax.rsqrt(jnp.mean(o * o, axis=1, keepdims=True) + SUBLN_EPS)
        o_ref[0, :, g * LANES:(g + 1) * LANES] = (o * g_ref[...] * (1.0 - lambda_init)).astype(o_ref.dtype)


def _diff_attention(q, k, vt, onehot, lam_params, subln_g, lambda_init, d_model):
    b, s, _ = q.shape
    gw = ATT_GROUPS * LANES
    n_steps = d_model // gw
    n_blocks = s // ATT_BLK
    return pl.pallas_call(
        functools.partial(_diff_kernel, lambda_init=lambda_init),
        grid=(b, n_steps, n_blocks),
        in_specs=[
            pl.BlockSpec((1, ATT_BLK, gw), lambda bi, h, i: (bi, i, h)),
            pl.BlockSpec((1, s, gw), lambda bi, h, i: (bi, 0, h)),
            pl.BlockSpec((1, n_blocks, gw, ATT_BLK), lambda bi, h, i: (bi, 0, h, 0)),
            pl.BlockSpec((s, LANES), lambda bi, h, i: (0, 0)),
            pl.BlockSpec(lam_params.shape, lambda bi, h, i: (0, 0)),
            pl.BlockSpec((1, LANES), lambda bi, h, i: (0, 0)),
        ],
        out_specs=pl.BlockSpec((1, ATT_BLK, gw), lambda bi, h, i: (bi, i, h)),
        out_shape=jax.ShapeDtypeStruct((b, s, d_model), BF16),
        scratch_shapes=_attn_scratch(),
        compiler_params=_cparams(("parallel", "parallel", "arbitrary")),
        name="diff_attn",
    )(q, k, vt, onehot, lam_params, subln_g)


def _layer_norm(z, g, b):
    mu = jnp.mean(z, axis=1, keepdims=True)
    zc = z - mu
    var = jnp.mean(zc * zc, axis=1, keepdims=True)
    return zc * lax.rsqrt(var + LN_EPS) * g + b


def _oproj_ln_kernel(x_ref, a_ref, w_ref, g_ref, b_ref, o_ref, *, alpha):
    y = jnp.dot(a_ref[...], w_ref[...], preferred_element_type=F32)
    o_ref[...] = _layer_norm(alpha * x_ref[...] + y, g_ref[...], b_ref[...])


def _oproj_ln(x2d, a2d, w_bf16, g, b, alpha):
    t, d = x2d.shape
    tm = PROJ_TM
    return pl.pallas_call(
        functools.partial(_oproj_ln_kernel, alpha=alpha),
        grid=(t // tm,),
        in_specs=[
            pl.BlockSpec((tm, d), lambda i: (i, 0)),
            pl.BlockSpec((tm, d), lambda i: (i, 0)),
            pl.BlockSpec((d, d), lambda i: (0, 0)),
            pl.BlockSpec((1, d), lambda i: (0, 0)),
            pl.BlockSpec((1, d), lambda i: (0, 0)),
        ],
        out_specs=pl.BlockSpec((tm, d), lambda i: (i, 0)),
        out_shape=jax.ShapeDtypeStruct((t, d), F32),
        compiler_params=_cparams(("parallel",)),
        name="oproj_ln",
    )(x2d, a2d, w_bf16, g, b)


def _gelu_tanh(x):
    return 0.5 * x * (1.0 + jnp.tanh(math.sqrt(2.0 / math.pi) * (x + 0.044715 * (x * x * x))))


def _ffn_kernel(x_ref, halo_ref, wu_ref, wg_ref, cw_ref, cb_ref, wo_ref, g_ref, b_ref, o_ref, acc_sc,
                *, alpha, tiles_per_seq):
    i = pl.program_id(0)
    j = pl.program_id(1)
    tm = x_ref.shape[0]
    x = x_ref[...]
    xb = x.astype(BF16)
    seq_start = (i % tiles_per_seq) == 0
    halo = jnp.where(seq_start, 0.0, halo_ref[...]).astype(BF16)
    x_ext = jnp.concatenate([halo, xb], axis=0)
    up = jnp.dot(x_ext, wu_ref[...], preferred_element_type=F32)
    gate = jnp.dot(xb, wg_ref[...], preferred_element_type=F32)
    cw = cw_ref[...]
    conv = (up[FFN_HALO - 2:FFN_HALO - 2 + tm] * cw[0:1]
            + up[FFN_HALO - 1:FFN_HALO - 1 + tm] * cw[1:2]
            + up[FFN_HALO:FFN_HALO + tm] * cw[2:3]
            + cb_ref[...])
    h = (_gelu_tanh(conv) * gate).astype(BF16)
    part = jnp.dot(h, wo_ref[...], preferred_element_type=F32)

    @pl.when(j == 0)
    def _():
        acc_sc[...] = part

    @pl.when(j > 0)
    def _():
        acc_sc[...] += part

    @pl.when(j == pl.num_programs(1) - 1)
    def _():
        o_ref[...] = _layer_norm(alpha * x + acc_sc[...], g_ref[...], b_ref[...])


def _ffn_tile(d_ff):
    n = d_ff // LANES
    for parts in range(2, n + 1):
        if n % parts == 0:
            return d_ff // parts
    return d_ff


def _conv_ffn_ln(x2d, w_in_bf16, conv_w, conv_b, w_out_bf16, g, b, alpha, seq):
    t, d = x2d.shape
    d_ff = w_out_bf16.shape[0]
    tm = FFN_TM
    tf = _ffn_tile(d_ff)
    nf = d_ff // tf
    halo_blocks = tm // FFN_HALO
    return pl.pallas_call(
        functools.partial(_ffn_kernel, alpha=alpha, tiles_per_seq=seq // tm),
        grid=(t // tm, nf),
        in_specs=[
            pl.BlockSpec((tm, d), lambda i, j: (i, 0)),
            pl.BlockSpec((FFN_HALO, d), lambda i, j: (jnp.maximum(i * halo_blocks - 1, 0), 0)),
            pl.BlockSpec((d, tf), lambda i, j: (0, j)),
            pl.BlockSpec((d, tf), lambda i, j: (0, nf + j)),
            pl.BlockSpec((CONV_WIDTH, tf), lambda i, j: (0, j)),
            pl.BlockSpec((1, tf), lambda i, j: (0, j)),
            pl.BlockSpec((tf, d), lambda i, j: (j, 0)),
            pl.BlockSpec((1, d), lambda i, j: (0, 0)),
            pl.BlockSpec((1, d), lambda i, j: (0, 0)),
        ],
        out_specs=pl.BlockSpec((tm, d), lambda i, j: (i, 0)),
        out_shape=jax.ShapeDtypeStruct((t, d), F32),
        scratch_shapes=[pltpu.VMEM((tm, d), F32)],
        compiler_params=_cparams(("parallel", "arbitrary")),
        name="conv_ffn_ln",
    )(x2d, x2d, w_in_bf16, w_in_bf16, conv_w, conv_b, w_out_bf16, g, b)


def _rotary_tables(positions):
    inv_freq = 1.0 / (ROPE_THETA ** (jnp.arange(ROPE_HALF, dtype=F32) * 2.0 / ROPE_ROT))
    ang = positions.astype(F32).reshape(-1, 1) * inv_freq
    cos, sin = jnp.cos(ang), jnp.sin(ang)
    t = ang.shape[0]
    ones = jnp.ones((t, HEAD_DIM - ROPE_ROT), F32)
    zeros = jnp.zeros((t, HEAD_DIM - ROPE_ROT), F32)
    c_head = jnp.concatenate([cos, cos, ones], axis=1)
    s_head = jnp.concatenate([-sin, sin, zeros], axis=1)
    reps = LANES // HEAD_DIM
    return jnp.tile(c_head, (1, reps)), jnp.tile(s_head, (1, reps))


def kernel(x, positions, a_w_qkv, a_w_o, w_kv_shared, b_w_q, b_w_o, b_lambda, b_subln_g, ln_g, ln_b,
           ffn_w_in, ffn_conv_w, ffn_conv_b, ffn_w_out):
    bsz, seq, d = x.shape
    depth = ffn_w_in.shape[0]
    n_a = a_w_qkv.shape[0]
    t = bsz * seq
    alpha = (2 * depth) ** 0.25
    assert d % (ATT_GROUPS * LANES) == 0 and seq % max(PROJ_TM, FFN_TM, ATT_CHUNK * ATT_BLK) == 0

    cos_tab, sin_tab = _rotary_tables(positions)
    onehot = (jnp.arange(seq)[:, None] // ATT_BLK == jnp.arange(LANES)[None, :]).astype(BF16)
    xs = x.reshape(t, d)
    k_sh = vt_sh = None
    for layer in range(depth):
        g0, b0 = ln_g[layer, 0].reshape(1, d), ln_b[layer, 0].reshape(1, d)
        g1, b1 = ln_g[layer, 1].reshape(1, d), ln_b[layer, 1].reshape(1, d)
        if layer < n_a:
            w_qkv = a_w_qkv[layer]
            qk = _proj_rot(xs, w_qkv[:, :2 * d].astype(BF16), cos_tab, sin_tab, tn=d, n_scaled_tiles=1)
            vt = _proj_t(xs.reshape(bsz, seq, d), w_qkv[:, 2 * d:].T.astype(BF16))
            o = _moba_attention(qk.reshape(bsz, seq, 2 * d), vt, onehot, d)
            xs = _oproj_ln(xs, o.reshape(t, d), a_w_o[layer].astype(BF16), g0, b0, alpha)
        else:
            jb = layer - n_a
            if jb == 0:
                k_sh = _proj_rot(xs, w_kv_shared[:, :d].astype(BF16), cos_tab, sin_tab, tn=d, n_scaled_tiles=0)
                k_sh = k_sh.reshape(bsz, seq, d)
                vt_sh = _proj_t(xs.reshape(bsz, seq, d), w_kv_shared[:, d:].T.astype(BF16))
            lambda_init = 0.8 - 0.6 * math.exp(-0.3 * layer)
            q = _proj_rot(xs, b_w_q[jb].astype(BF16), cos_tab, sin_tab, tn=d, n_scaled_tiles=1)
            o = _diff_attention(q.reshape(bsz, seq, d), k_sh, vt_sh, onehot, b_lambda[jb],
                                b_subln_g[jb].reshape(1, LANES), lambda_init, d)
            xs = _oproj_ln(xs, o.reshape(t, d), b_w_o[jb].astype(BF16), g0, b0, alpha)
        xs = _conv_ffn_ln(xs, ffn_w_in[layer].astype(BF16), ffn_conv_w[layer], ffn_conv_b[layer].reshape(1, -1),
                          ffn_w_out[layer].astype(BF16), g1, b1, alpha, seq)
    return xs.reshape(bsz, seq, d)
```

```python
import functools
import math

import jax
import jax.numpy as jnp
from jax import lax
from jax.experimental import pallas as pl
from jax.experimental.pallas import tpu as pltpu

F32 = jnp.float32
BF16 = jnp.bfloat16

HEAD_DIM = 64
LANES = 128
MOBA_BLOCK = 256
MOBA_TOPK = 3
ROPE_THETA = 500000.0
ROPE_ROT = HEAD_DIM // 4
ROPE_HALF = ROPE_ROT // 2
LN_EPS = 1e-5
SUBLN_EPS = 1e-5
CONV_WIDTH = 3
NEG = -1e30

VMEM_LIMIT = 56 * 1024 * 1024
PROJ_TM = 1024
PROJ_T_BLOCKS = 4
FFN_TM = 512
FFN_HALO = 16
ATT_BLK = MOBA_BLOCK
ATT_CHUNK = 4
ATT_GROUPS = 4
Q_SCALE = HEAD_DIM ** -0.5 * math.log2(math.e)

_NT = (((1,), (1,)), ((), ()))


def _cparams(sem):
    return pltpu.CompilerParams(dimension_semantics=sem, vmem_limit_bytes=VMEM_LIMIT)


def _proj_rot_kernel(x_ref, w_ref, c_ref, s_ref, o_ref, *, n_scaled_tiles):
    xb = x_ref[...].astype(BF16)
    y = jnp.dot(xb, w_ref[...], preferred_element_type=F32)
    y = y * jnp.where(pl.program_id(1) < n_scaled_tiles, Q_SCALE, 1.0)
    tm, tn = y.shape
    c = c_ref[...]
    s = s_ref[...]
    lane = lax.broadcasted_iota(jnp.int32, (tm, LANES), 1)
    first_half = (lane & (HEAD_DIM - 1)) < ROPE_HALF
    cols = []
    for g in range(tn // LANES):
        yg = y[:, g * LANES:(g + 1) * LANES]
        partner = jnp.where(first_half, pltpu.roll(yg, LANES - ROPE_HALF, 1), pltpu.roll(yg, ROPE_HALF, 1))
        cols.append(yg * c + partner * s)
    o_ref[...] = jnp.concatenate(cols, axis=1).astype(o_ref.dtype)


def _proj_rot(x2d, w_bf16, cos_tab, sin_tab, tn, n_scaled_tiles):
    t, k = x2d.shape
    n = w_bf16.shape[1]
    tm = PROJ_TM
    return pl.pallas_call(
        functools.partial(_proj_rot_kernel, n_scaled_tiles=n_scaled_tiles),
        grid=(t // tm, n // tn),
        in_specs=[
            pl.BlockSpec((tm, k), lambda i, j: (i, 0)),
            pl.BlockSpec((k, tn), lambda i, j: (0, j)),
            pl.BlockSpec((tm, LANES), lambda i, j: (i, 0)),
            pl.BlockSpec((tm, LANES), lambda i, j: (i, 0)),
        ],
        out_specs=pl.BlockSpec((tm, tn), lambda i, j: (i, j)),
        out_shape=jax.ShapeDtypeStruct((t, n), BF16),
        compiler_params=_cparams(("parallel", "arbitrary")),
        name="proj_rot",
    )(x2d, w_bf16, cos_tab, sin_tab)


def _proj_t_kernel(x_ref, wt_ref, o_ref):
    xb = x_ref[0].astype(BF16)
    y = lax.dot_general(wt_ref[...], xb, _NT, preferred_element_type=F32)
    for j in range(PROJ_T_BLOCKS):
        o_ref[0, j] = y[:, j * ATT_BLK:(j + 1) * ATT_BLK].astype(o_ref.dtype)


def _proj_t(x3d, wt_bf16):
    b, s, d = x3d.shape
    n = wt_bf16.shape[0]
    rows = PROJ_T_BLOCKS * ATT_BLK
    return pl.pallas_call(
        _proj_t_kernel,
        grid=(b, s // rows),
        in_specs=[
            pl.BlockSpec((1, rows, d), lambda bi, j: (bi, j, 0)),
            pl.BlockSpec((n, d), lambda bi, j: (0, 0)),
        ],
        out_specs=pl.BlockSpec((1, PROJ_T_BLOCKS, n, ATT_BLK), lambda bi, j: (bi, j, 0, 0)),
        out_shape=jax.ShapeDtypeStruct((b, s // ATT_BLK, n, ATT_BLK), BF16),
        compiler_params=_cparams(("parallel", "parallel")),
        name="proj_t",
    )(x3d, wt_bf16)


def _split_heads(q2):
    lane = lax.broadcasted_iota(jnp.int32, q2.shape, 1)
    zero = jnp.zeros_like(q2)
    q_a = jnp.where(lane < HEAD_DIM, q2, zero)
    q_b = jnp.where(lane >= HEAD_DIM, q2, zero)
    return jnp.concatenate([q_a, q_b], axis=0)


def _attend(q_aug, k_ref, vt_ref, e_ref, i, m_sc, l_sc, acc_sc):
    tk = ATT_CHUNK * ATT_BLK
    groups = range(len(q_aug))

    def scores(g, start, size):
        k_c = k_ref[0, pl.ds(start, size), g * LANES:(g + 1) * LANES]
        if e_ref is not None:
            k_c = jnp.concatenate([k_c, e_ref[pl.ds(start, size), :]], axis=1)
        return lax.dot_general(k_c, q_aug[g], _NT, preferred_element_type=F32)

    def vt_cols(g, blk0, n):
        return jnp.concatenate([vt_ref[0, blk0 + b, g * LANES:(g + 1) * LANES, :] for b in range(n)], axis=1)

    def tail(n_tail):
        blk0 = i - n_tail
        start = pl.multiple_of(blk0 * ATT_BLK, ATT_BLK)
        for g in groups:
            s_t = scores(g, start, (n_tail + 1) * ATT_BLK)
            s_own = s_t[n_tail * ATT_BLK:]
            key_pos = lax.broadcasted_iota(jnp.int32, s_own.shape, 0)
            q_pos = lax.broadcasted_iota(jnp.int32, s_own.shape, 1) & (ATT_BLK - 1)
            s_own = jnp.where(key_pos <= q_pos, s_own, NEG)
            s_t = jnp.concatenate([s_t[:n_tail * ATT_BLK], s_own], axis=0) if n_tail else s_own
            m = jnp.max(s_t, axis=0, keepdims=True)
            p = jnp.exp2(s_t - m)
            m_sc[g] = m
            l_sc[g] = jnp.sum(p, axis=0, keepdims=True)
            acc_sc[g] = jnp.dot(vt_cols(g, blk0, n_tail + 1), p.astype(BF16), preferred_element_type=F32)

    for n_tail in range(ATT_CHUNK):
        pl.when((i & (ATT_CHUNK - 1)) == n_tail)(functools.partial(tail, n_tail))

    def body(c, carry):
        start = pl.multiple_of(c * tk, tk)
        s_next = scores(0, start, tk)
        for g in groups:
            s_g, s_next = s_next, (scores(g + 1, start, tk) if g + 1 < len(q_aug) else None)
            m_old = m_sc[g]
            m_new = jnp.maximum(m_old, jnp.max(s_g, axis=0, keepdims=True))
            alpha = jnp.exp2(m_old - m_new)
            p = jnp.exp2(s_g - m_new)
            l_sc[g] = alpha * l_sc[g] + jnp.sum(p, axis=0, keepdims=True)
            pv = jnp.dot(vt_cols(g, c * ATT_CHUNK, ATT_CHUNK), p.astype(BF16), preferred_element_type=F32)
            acc_sc[g] = alpha * acc_sc[g] + pv
            m_sc[g] = m_new
        return carry

    lax.fori_loop(0, i // ATT_CHUNK, body, 0)
    return [acc_sc[g] / l_sc[g] for g in groups]


def _attn_scratch():
    return [
        pltpu.VMEM((ATT_GROUPS, 1, 2 * ATT_BLK), F32),
        pltpu.VMEM((ATT_GROUPS, 1, 2 * ATT_BLK), F32),
        pltpu.VMEM((ATT_GROUPS, LANES, 2 * ATT_BLK), F32),
    ]


def _moba_kernel(q_ref, k_ref, vt_ref, e_ref, o_ref, kmean_sc, m_sc, l_sc, acc_sc, *, n_blocks):
    i = pl.program_id(2)
    tq = ATT_BLK

    @pl.when(i == 0)
    def _():
        for g in range(ATT_GROUPS):
            for j in range(n_blocks):
                kj = k_ref[0, j * MOBA_BLOCK:(j + 1) * MOBA_BLOCK, g * LANES:(g + 1) * LANES].astype(F32)
                kmean_sc[g, j:j + 1, :] = jnp.mean(kj, axis=0, keepdims=True)

    q_aug = []
    for g in range(ATT_GROUPS):
        qs = _split_heads(q_ref[0, :, g * LANES:(g + 1) * LANES])
        gate_t = lax.dot_general(kmean_sc[g].astype(BF16), qs, _NT, preferred_element_type=F32)
        blk = lax.broadcasted_iota(jnp.int32, gate_t.shape, 0)
        cnt = jnp.zeros(gate_t.shape, F32)
        for jp in range(n_blocks):
            row = gate_t[jp:jp + 1, :]
            beats = (row > gate_t) | ((row == gate_t) & (jp < blk))
            cnt = cnt + jnp.where(beats & (jp < i), 1.0, 0.0)
        selected = (cnt < MOBA_TOPK) & (blk < i)
        bias_t = jnp.where(selected | (blk == i), 0.0, NEG)
        bias_pad = jnp.concatenate([bias_t, jnp.full((LANES - n_blocks, 2 * tq), NEG, F32)], axis=0)
        q_aug.append(jnp.concatenate([qs, bias_pad.T.astype(BF16)], axis=1))

    o_t = _attend(q_aug, k_ref, vt_ref, e_ref, i, m_sc, l_sc, acc_sc)
    for g in range(ATT_GROUPS):
        o_sel = jnp.concatenate([o_t[g][:HEAD_DIM, :tq], o_t[g][HEAD_DIM:, tq:]], axis=0)
        o_ref[0, :, g * LANES:(g + 1) * LANES] = o_sel.T.astype(o_ref.dtype)


def _moba_attention(qk, vt, onehot, d_model):
    b, s, _ = qk.shape
    gw = ATT_GROUPS * LANES
    n_steps = d_model // gw
    n_blocks = s // MOBA_BLOCK
    return pl.pallas_call(
        functools.partial(_moba_kernel, n_blocks=n_blocks),
        grid=(b, n_steps, n_blocks),
        in_specs=[
            pl.BlockSpec((1, ATT_BLK, gw), lambda bi, g, i: (bi, i, g)),
            pl.BlockSpec((1, s, gw), lambda bi, g, i: (bi, 0, n_steps + g)),
            pl.BlockSpec((1, n_blocks, gw, ATT_BLK), lambda bi, g, i: (bi, 0, g, 0)),
            pl.BlockSpec((s, LANES), lambda bi, g, i: (0, 0)),
        ],
        out_specs=pl.BlockSpec((1, ATT_BLK, gw), lambda bi, g, i: (bi, i, g)),
        out_shape=jax.ShapeDtypeStruct((b, s, d_model), BF16),
        scratch_shapes=[pltpu.VMEM((ATT_GROUPS, n_blocks, LANES), F32)] + _attn_scratch(),
        compiler_params=_cparams(("parallel", "parallel", "arbitrary")),
        name="moba_attn",
    )(qk, qk, vt, onehot)


def _diff_kernel(q_ref, k_ref, vt_ref, lam_ref, g_ref, o_ref, m_sc, l_sc, acc_sc, *, lambda_init):
    i = pl.program_id(2)
    tq = ATT_BLK
    q_rows = [_split_heads(q_ref[0, :, g * LANES:(g + 1) * LANES])
              for g in range(ATT_GROUPS)]
    o_t = _attend(q_rows, k_ref, vt_ref, None, i, m_sc, l_sc, acc_sc)

    lp = lam_ref[...].astype(F32)
    lam = (jnp.exp(jnp.sum(lp[0:1] * lp[1:2], axis=1, keepdims=True))
           - jnp.exp(jnp.sum(lp[2:3] * lp[3:4], axis=1, keepdims=True)) + lambda_init)
    for g in range(ATT_GROUPS):
        o = (o_t[g][:, :tq] - lam * o_t[g][:, tq:]).T
        o = o * lax.rsqrt(jnp.mean(o * o, axis=1, keepdims=True) + SUBLN_EPS)
        o_ref[0, :, g * LANES:(g + 1) * LANES] = (o * g_ref[...] * (1.0 - lambda_init)).astype(o_ref.dtype)


def _diff_attention(q, k, vt, lam_params, subln_g, lambda_init, d_model):
    b, s, _ = q.shape
    gw = ATT_GROUPS * LANES
    n_steps = d_model // gw
    n_blocks = s // ATT_BLK
    return pl.pallas_call(
        functools.partial(_diff_kernel, lambda_init=lambda_init),
        grid=(b, n_steps, n_blocks),
        in_specs=[
            pl.BlockSpec((1, ATT_BLK, gw), lambda bi, h, i: (bi, i, h)),
            pl.BlockSpec((1, s, gw), lambda bi, h, i: (bi, 0, h)),
            pl.BlockSpec((1, n_blocks, gw, ATT_BLK), lambda bi, h, i: (bi, 0, h, 0)),
            pl.BlockSpec(lam_params.shape, lambda bi, h, i: (0, 0)),
            pl.BlockSpec((1, LANES), lambda bi, h, i: (0, 0)),
        ],
        out_specs=pl.BlockSpec((1, ATT_BLK, gw), lambda bi, h, i: (bi, i, h)),
        out_shape=jax.ShapeDtypeStruct((b, s, d_model), BF16),
        scratch_shapes=_attn_scratch(),
        compiler_params=_cparams(("parallel", "parallel", "arbitrary")),
        name="diff_attn",
    )(q, k, vt, lam_params, subln_g)


def _layer_norm(z, g, b):
    mu = jnp.mean(z, axis=1, keepdims=True)
    zc = z - mu
    var = jnp.mean(zc * zc, axis=1, keepdims=True)
    return zc * lax.rsqrt(var + LN_EPS) * g + b


def _oproj_ln_kernel(x_ref, a_ref, w_ref, g_ref, b_ref, o_ref, *, alpha):
    y = jnp.dot(a_ref[...], w_ref[...], preferred_element_type=F32)
    o_ref[...] = _layer_norm(alpha * x_ref[...] + y, g_ref[...], b_ref[...])


def _oproj_ln(x2d, a2d, w_bf16, g, b, alpha):
    t, d = x2d.shape
    tm = PROJ_TM
    return pl.pallas_call(
        functools.partial(_oproj_ln_kernel, alpha=alpha),
        grid=(t // tm,),
        in_specs=[
            pl.BlockSpec((tm, d), lambda i: (i, 0)),
            pl.BlockSpec((tm, d), lambda i: (i, 0)),
            pl.BlockSpec((d, d), lambda i: (0, 0)),
            pl.BlockSpec((1, d), lambda i: (0, 0)),
            pl.BlockSpec((1, d), lambda i: (0, 0)),
        ],
        out_specs=pl.BlockSpec((tm, d), lambda i: (i, 0)),
        out_shape=jax.ShapeDtypeStruct((t, d), F32),
        compiler_params=_cparams(("parallel",)),
        name="oproj_ln",
    )(x2d, a2d, w_bf16, g, b)


def _gelu_tanh(x):
    return 0.5 * x * (1.0 + jnp.tanh(math.sqrt(2.0 / math.pi) * (x + 0.044715 * (x * x * x))))


def _ffn_kernel(x_ref, halo_ref, wu_ref, wg_ref, cw_ref, cb_ref, wo_ref, g_ref, b_ref, o_ref, acc_sc,
                *, alpha, tiles_per_seq):
    i = pl.program_id(0)
    j = pl.program_id(1)
    tm = x_ref.shape[0]
    x = x_ref[...]
    xb = x.astype(BF16)
    seq_start = (i % tiles_per_seq) == 0
    halo = jnp.where(seq_start, 0.0, halo_ref[...]).astype(BF16)
    x_ext = jnp.concatenate([halo, xb], axis=0)
    up = jnp.dot(x_ext, wu_ref[...], preferred_element_type=F32)
    gate = jnp.dot(xb, wg_ref[...], preferred_element_type=F32)
    cw = cw_ref[...]
    conv = (up[FFN_HALO - 2:FFN_HALO - 2 + tm] * cw[0:1]
            + up[FFN_HALO - 1:FFN_HALO - 1 + tm] * cw[1:2]
            + up[FFN_HALO:FFN_HALO + tm] * cw[2:3]
            + cb_ref[...])
    h = (_gelu_tanh(conv) * gate).astype(BF16)
    part = jnp.dot(h, wo_ref[...], preferred_element_type=F32)

    @pl.when(j == 0)
    def _():
        acc_sc[...] = part

    @pl.when(j > 0)
    def _():
        acc_sc[...] += part

    @pl.when(j == pl.num_programs(1) - 1)
    def _():
        o_ref[...] = _layer_norm(alpha * x + acc_sc[...], g_ref[...], b_ref[...])


def _ffn_tile(d_ff):
    n = d_ff // LANES
    for parts in range(2, n + 1):
        if n % parts == 0:
            return d_ff // parts
    return d_ff


def _conv_ffn_ln(x2d, w_in_bf16, conv_w, conv_b, w_out_bf16, g, b, alpha, seq):
    t, d = x2d.shape
    d_ff = w_out_bf16.shape[0]
    tm = FFN_TM
    tf = _ffn_tile(d_ff)
    nf = d_ff // tf
    halo_blocks = tm // FFN_HALO
    return pl.pallas_call(
        functools.partial(_ffn_kernel, alpha=alpha, tiles_per_seq=seq // tm),
        grid=(t // tm, nf),
        in_specs=[
            pl.BlockSpec((tm, d), lambda i, j: (i, 0)),
            pl.BlockSpec((FFN_HALO, d), lambda i, j: (jnp.maximum(i * halo_blocks - 1, 0), 0)),
            pl.BlockSpec((d, tf), lambda i, j: (0, j)),
            pl.BlockSpec((d, tf), lambda i, j: (0, nf + j)),
            pl.BlockSpec((CONV_WIDTH, tf), lambda i, j: (0, j)),
            pl.BlockSpec((1, tf), lambda i, j: (0, j)),
            pl.BlockSpec((tf, d), lambda i, j: (j, 0)),
            pl.BlockSpec((1, d), lambda i, j: (0, 0)),
            pl.BlockSpec((1, d), lambda i, j: (0, 0)),
        ],
        out_specs=pl.BlockSpec((tm, d), lambda i, j: (i, 0)),
        out_shape=jax.ShapeDtypeStruct((t, d), F32),
        scratch_shapes=[pltpu.VMEM((tm, d), F32)],
        compiler_params=_cparams(("parallel", "arbitrary")),
        name="conv_ffn_ln",
    )(x2d, x2d, w_in_bf16, w_in_bf16, conv_w, conv_b, w_out_bf16, g, b)


def _rotary_tables(positions):
    inv_freq = 1.0 / (ROPE_THETA ** (jnp.arange(ROPE_HALF, dtype=F32) * 2.0 / ROPE_ROT))
    ang = positions.astype(F32).reshape(-1, 1) * inv_freq
    cos, sin = jnp.cos(ang), jnp.sin(ang)
    t = ang.shape[0]
    ones = jnp.ones((t, HEAD_DIM - ROPE_ROT), F32)
    zeros = jnp.zeros((t, HEAD_DIM - ROPE_ROT), F32)
    c_head = jnp.concatenate([cos, cos, ones], axis=1)
    s_head = jnp.concatenate([-sin, sin, zeros], axis=1)
    reps = LANES // HEAD_DIM
    return jnp.tile(c_head, (1, reps)), jnp.tile(s_head, (1, reps))


def kernel(x, positions, a_w_qkv, a_w_o, w_kv_shared, b_w_q, b_w_o, b_lambda, b_subln_g, ln_g, ln_b,
           ffn_w_in, ffn_conv_w, ffn_conv_b, ffn_w_out):
    bsz, seq, d = x.shape
    depth = ffn_w_in.shape[0]
    n_a = a_w_qkv.shape[0]
    t = bsz * seq
    alpha = (2 * depth) ** 0.25
    assert d % (ATT_GROUPS * LANES) == 0
    assert seq % max(PROJ_TM, FFN_TM, ATT_CHUNK * ATT_BLK, PROJ_T_BLOCKS * ATT_BLK) == 0

    cos_tab, sin_tab = _rotary_tables(positions)
    onehot = (jnp.arange(seq)[:, None] // ATT_BLK == jnp.arange(LANES)[None, :]).astype(BF16)
    xs = x.reshape(t, d)
    k_sh = vt_sh = None
    for layer in range(depth):
        g0, b0 = ln_g[layer, 0].reshape(1, d), ln_b[layer, 0].reshape(1, d)
        g1, b1 = ln_g[layer, 1].reshape(1, d), ln_b[layer, 1].reshape(1, d)
        if layer < n_a:
            w_qkv = a_w_qkv[layer]
            qk = _proj_rot(xs, w_qkv[:, :2 * d].astype(BF16), cos_tab, sin_tab, tn=d, n_scaled_tiles=1)
            vt = _proj_t(xs.reshape(bsz, seq, d), w_qkv[:, 2 * d:].T.astype(BF16))
            o = _moba_attention(qk.reshape(bsz, seq, 2 * d), vt, onehot, d)
            xs = _oproj_ln(xs, o.reshape(t, d), a_w_o[layer].astype(BF16), g0, b0, alpha)
        else:
            jb = layer - n_a
            if jb == 0:
                k_sh = _proj_rot(xs, w_kv_shared[:, :d].astype(BF16), cos_tab, sin_tab, tn=d, n_scaled_tiles=0)
                k_sh = k_sh.reshape(bsz, seq, d)
                vt_sh = _proj_t(xs.reshape(bsz, seq, d), w_kv_shared[:, d:].T.astype(BF16))
            lambda_init = 0.8 - 0.6 * math.exp(-0.3 * layer)
            q = _proj_rot(xs, b_w_q[jb].astype(BF16), cos_tab, sin_tab, tn=d, n_scaled_tiles=1)
            o = _diff_attention(q.reshape(bsz, seq, d), k_sh, vt_sh, b_lambda[jb],
                                b_subln_g[jb].reshape(1, LANES), lambda_init, d)
            xs = _oproj_ln(xs, o.reshape(t, d), b_w_o[jb].astype(BF16), g0, b0, alpha)
        xs = _conv_ffn_ln(xs, ffn_w_in[layer].astype(BF16), ffn_conv_w[layer], ffn_conv_b[layer].reshape(1, -1),
                          ffn_w_out[layer].astype(BF16), g1, b1, alpha, seq)
    return xs.reshape(bsz, seq, d)
```

```python
import functools
import math

import jax
import jax.numpy as jnp
from jax import lax
from jax.experimental import pallas as pl
from jax.experimental.pallas import tpu as pltpu

F32 = jnp.float32
BF16 = jnp.bfloat16

HEAD_DIM = 64
LANES = 128
SUBLANES = 8
MOBA_BLOCK = 256
MOBA_TOPK = 3
ROPE_THETA = 500000.0
ROPE_ROT = HEAD_DIM // 4
ROPE_HALF = ROPE_ROT // 2
LN_EPS = 1e-5
SUBLN_EPS = 1e-5
CONV_WIDTH = 3
NEG = -1e30

VMEM_LIMIT = 56 * 1024 * 1024
PROJ_TM = 1024
PROJ_T_BLOCKS = 4
FFN_TM = 512
FFN_HALO = 16
ATT_BLK = MOBA_BLOCK
ATT_CHUNK = 4
ATT_GROUPS = 4
Q_SCALE = HEAD_DIM ** -0.5 * math.log2(math.e)
BOUND_SLACK = 1.05
FAST_BOUND = 32.0

_NT = (((1,), (1,)), ((), ()))


def _cparams(sem):
    return pltpu.CompilerParams(dimension_semantics=sem, vmem_limit_bytes=VMEM_LIMIT)


def _proj_rot_kernel(x_ref, w_ref, c_ref, s_ref, o_ref, *, n_scaled_tiles):
    xb = x_ref[...].astype(BF16)
    y = jnp.dot(xb, w_ref[...], preferred_element_type=F32)
    y = y * jnp.where(pl.program_id(1) < n_scaled_tiles, Q_SCALE, 1.0)
    tm, tn = y.shape
    c = c_ref[...]
    s = s_ref[...]
    lane = lax.broadcasted_iota(jnp.int32, (tm, LANES), 1)
    first_half = (lane & (HEAD_DIM - 1)) < ROPE_HALF
    cols = []
    for g in range(tn // LANES):
        yg = y[:, g * LANES:(g + 1) * LANES]
        partner = jnp.where(first_half, pltpu.roll(yg, LANES - ROPE_HALF, 1), pltpu.roll(yg, ROPE_HALF, 1))
        cols.append(yg * c + partner * s)
    o_ref[...] = jnp.concatenate(cols, axis=1).astype(o_ref.dtype)


def _proj_rot(x2d, w_all, layer, n, cos_tab, sin_tab, tn, n_scaled_tiles):
    t, k = x2d.shape
    tm = PROJ_TM
    return pl.pallas_call(
        functools.partial(_proj_rot_kernel, n_scaled_tiles=n_scaled_tiles),
        grid=(t // tm, n // tn),
        in_specs=[
            pl.BlockSpec((tm, k), lambda i, j: (i, 0)),
            pl.BlockSpec((None, k, tn), lambda i, j: (layer, 0, j)),
            pl.BlockSpec((tm, LANES), lambda i, j: (i, 0)),
            pl.BlockSpec((tm, LANES), lambda i, j: (i, 0)),
        ],
        out_specs=pl.BlockSpec((tm, tn), lambda i, j: (i, j)),
        out_shape=jax.ShapeDtypeStruct((t, n), BF16),
        compiler_params=_cparams(("parallel", "arbitrary")),
        name="proj_rot",
    )(x2d, w_all, cos_tab, sin_tab)


def _proj_t_kernel(x_ref, wt_ref, o_ref):
    xb = x_ref[0].astype(BF16)
    y = lax.dot_general(wt_ref[...], xb, _NT, preferred_element_type=F32)
    for j in range(PROJ_T_BLOCKS):
        o_ref[0, j] = y[:, j * ATT_BLK:(j + 1) * ATT_BLK].astype(o_ref.dtype)


def _proj_t(x3d, wt_all, layer):
    b, s, d = x3d.shape
    n = wt_all.shape[1]
    rows = PROJ_T_BLOCKS * ATT_BLK
    return pl.pallas_call(
        _proj_t_kernel,
        grid=(b, s // rows),
        in_specs=[
            pl.BlockSpec((1, rows, d), lambda bi, j: (bi, j, 0)),
            pl.BlockSpec((None, n, d), lambda bi, j: (layer, 0, 0)),
        ],
        out_specs=pl.BlockSpec((1, PROJ_T_BLOCKS, n, ATT_BLK), lambda bi, j: (bi, j, 0, 0)),
        out_shape=jax.ShapeDtypeStruct((b, s // ATT_BLK, n, ATT_BLK), BF16),
        compiler_params=_cparams(("parallel", "parallel")),
        name="proj_t",
    )(x3d, wt_all)


def _split_heads(q2):
    lane = lax.broadcasted_iota(jnp.int32, q2.shape, 1)
    zero = jnp.zeros_like(q2)
    q_a = jnp.where(lane < HEAD_DIM, q2, zero)
    q_b = jnp.where(lane >= HEAD_DIM, q2, zero)
    return jnp.concatenate([q_a, q_b], axis=0)


def _head_selector():
    r = lax.broadcasted_iota(jnp.int32, (SUBLANES, LANES), 0)
    lane = lax.broadcasted_iota(jnp.int32, (SUBLANES, LANES), 1)
    return jnp.where(((r == 0) & (lane < HEAD_DIM)) | ((r == 1) & (lane >= HEAD_DIM)), 1.0, 0.0).astype(BF16)


def _sq_norms_t(sel, x):
    xf = x.astype(F32)
    return lax.dot_general(sel, (xf * xf).astype(BF16), _NT, preferred_element_type=F32)


def _store_key_norms(k_ref, knm_sc, n_blocks):
    sel = _head_selector()
    for g in range(ATT_GROUPS):
        kn2 = jnp.zeros((SUBLANES, 1), F32)
        for j in range(n_blocks):
            kj = k_ref[0, j * ATT_BLK:(j + 1) * ATT_BLK, g * LANES:(g + 1) * LANES]
            kn2 = jnp.maximum(kn2, jnp.max(_sq_norms_t(sel, kj), axis=1, keepdims=True))
        knm_sc[g] = jnp.broadcast_to(kn2, (SUBLANES, LANES))


def _attend(q_rows, q_aug, k_ref, vt_ref, e_ref, knm_sc, i, m_sc, l_sc, acc_sc):
    tk = ATT_CHUNK * ATT_BLK
    groups = range(len(q_aug))
    n_cols = 2 * ATT_BLK

    def scores(g, start, size):
        k_c = k_ref[0, pl.ds(start, size), g * LANES:(g + 1) * LANES]
        if e_ref is not None:
            k_c = jnp.concatenate([k_c, e_ref[pl.ds(start, size), :]], axis=1)
        return lax.dot_general(k_c, q_aug[g], _NT, preferred_element_type=F32)

    def vt_cols(g, blk0, n):
        return jnp.concatenate([vt_ref[0, blk0 + b, g * LANES:(g + 1) * LANES, :] for b in range(n)], axis=1)

    def causal(s_own):
        key_pos = lax.broadcasted_iota(jnp.int32, s_own.shape, 0)
        q_pos = lax.broadcasted_iota(jnp.int32, s_own.shape, 1) & (ATT_BLK - 1)
        return jnp.where(key_pos <= q_pos, s_own, NEG)

    def sublane_sums(p):
        return jnp.sum(p.reshape(-1, SUBLANES, n_cols), axis=0)

    ones_row = (lax.broadcasted_iota(jnp.int32, (SUBLANES, LANES), 0) == 0).astype(BF16)
    col = lax.broadcasted_iota(jnp.int32, (1, n_cols), 1)
    bound = []
    for g in groups:
        qn2 = _sq_norms_t(ones_row, q_rows[g])[0:1]
        kn2 = jnp.where(col < ATT_BLK, knm_sc[g, 0:1, 0:1], knm_sc[g, 1:2, 0:1])
        bound.append(jnp.sqrt(qn2 * kn2) * BOUND_SLACK)
    worst = functools.reduce(jnp.maximum, [jnp.max(b) for b in bound])
    fast = worst <= FAST_BOUND

    def fast_tail(n_tail):
        blk0 = i - n_tail
        start = pl.multiple_of(blk0 * ATT_BLK, ATT_BLK)
        for g in groups:
            s_t = scores(g, start, (n_tail + 1) * ATT_BLK)
            s_own = causal(s_t[n_tail * ATT_BLK:])
            s_t = jnp.concatenate([s_t[:n_tail * ATT_BLK], s_own], axis=0) if n_tail else s_own
            p = jnp.exp2(s_t - bound[g])
            l_sc[g] = sublane_sums(p)
            acc_sc[g] = jnp.dot(vt_cols(g, blk0, n_tail + 1), p.astype(BF16), preferred_element_type=F32)

    def fast_body(c, carry):
        start = pl.multiple_of(c * tk, tk)
        for g in groups:
            p = jnp.exp2(scores(g, start, tk) - bound[g])
            l_sc[g] += sublane_sums(p)
            acc_sc[g] += jnp.dot(vt_cols(g, c * ATT_CHUNK, ATT_CHUNK), p.astype(BF16),
                                 preferred_element_type=F32)
        return carry

    @pl.when(fast)
    def _():
        for n_tail in range(ATT_CHUNK):
            pl.when((i & (ATT_CHUNK - 1)) == n_tail)(functools.partial(fast_tail, n_tail))
        lax.fori_loop(0, i // ATT_CHUNK, fast_body, 0)
        for g in groups:
            acc_sc[g] = acc_sc[g] / jnp.sum(l_sc[g], axis=0, keepdims=True)

    @pl.when(jnp.logical_not(fast))
    def _():
        own = pl.multiple_of(i * ATT_BLK, ATT_BLK)
        for g in groups:
            s_t = causal(scores(g, own, ATT_BLK))
            m = jnp.max(s_t, axis=0, keepdims=True)
            p = jnp.exp2(s_t - m)
            m_sc[g] = m
            l_sc[g, 0:1, :] = jnp.sum(p, axis=0, keepdims=True)
            acc_sc[g] = jnp.dot(vt_cols(g, i, 1), p.astype(BF16), preferred_element_type=F32)

        def exact_body(j, carry):
            start = pl.multiple_of(j * ATT_BLK, ATT_BLK)
            for g in groups:
                s_j = scores(g, start, ATT_BLK)
                m_old = m_sc[g]
                m_new = jnp.maximum(m_old, jnp.max(s_j, axis=0, keepdims=True))
                alpha = jnp.exp2(m_old - m_new)
                p = jnp.exp2(s_j - m_new)
                l_sc[g, 0:1, :] = alpha * l_sc[g, 0:1, :] + jnp.sum(p, axis=0, keepdims=True)
                acc_sc[g] = alpha * acc_sc[g] + jnp.dot(vt_cols(g, j, 1), p.astype(BF16),
                                                        preferred_element_type=F32)
                m_sc[g] = m_new
            return carry

        lax.fori_loop(0, i, exact_body, 0)
        for g in groups:
            acc_sc[g] = acc_sc[g] / l_sc[g, 0:1, :]

    return [acc_sc[g] for g in groups]


def _attn_scratch():
    return [
        pltpu.VMEM((ATT_GROUPS, SUBLANES, LANES), F32),
        pltpu.VMEM((ATT_GROUPS, 1, 2 * ATT_BLK), F32),
        pltpu.VMEM((ATT_GROUPS, SUBLANES, 2 * ATT_BLK), F32),
        pltpu.VMEM((ATT_GROUPS, LANES, 2 * ATT_BLK), F32),
    ]


def _moba_kernel(q_ref, k_ref, vt_ref, e_ref, o_ref, kmean_sc, knm_sc, m_sc, l_sc, acc_sc, *, n_blocks):
    i = pl.program_id(2)
    tq = ATT_BLK

    @pl.when(i == 0)
    def _():
        _store_key_norms(k_ref, knm_sc, n_blocks)
        for g in range(ATT_GROUPS):
            for j in range(n_blocks):
                kj = k_ref[0, j * MOBA_BLOCK:(j + 1) * MOBA_BLOCK, g * LANES:(g + 1) * LANES].astype(F32)
                kmean_sc[g, j:j + 1, :] = jnp.mean(kj, axis=0, keepdims=True)

    q_rows, q_aug = [], []
    for g in range(ATT_GROUPS):
        qs = _split_heads(q_ref[0, :, g * LANES:(g + 1) * LANES])
        q_rows.append(qs)
        gate_t = lax.dot_general(kmean_sc[g].astype(BF16), qs, _NT, preferred_element_type=F32)
        blk = lax.broadcasted_iota(jnp.int32, gate_t.shape, 0)
        cnt = jnp.zeros(gate_t.shape, F32)
        for jp in range(n_blocks):
            row = gate_t[jp:jp + 1, :]
            beats = (row > gate_t) | ((row == gate_t) & (jp < blk))
            cnt = cnt + jnp.where(beats & (jp < i), 1.0, 0.0)
        selected = (cnt < MOBA_TOPK) & (blk < i)
        bias_t = jnp.where(selected | (blk == i), 0.0, NEG)
        bias_pad = jnp.concatenate([bias_t, jnp.full((LANES - n_blocks, 2 * tq), NEG, F32)], axis=0)
        q_aug.append(jnp.concatenate([qs, bias_pad.T.astype(BF16)], axis=1))

    o_t = _attend(q_rows, q_aug, k_ref, vt_ref, e_ref, knm_sc, i, m_sc, l_sc, acc_sc)
    for g in range(ATT_GROUPS):
        o_sel = jnp.concatenate([o_t[g][:HEAD_DIM, :tq], o_t[g][HEAD_DIM:, tq:]], axis=0)
        o_ref[0, :, g * LANES:(g + 1) * LANES] = o_sel.T.astype(o_ref.dtype)


def _moba_attention(qk, vt, onehot, d_model):
    b, s, _ = qk.shape
    gw = ATT_GROUPS * LANES
    n_steps = d_model // gw
    n_blocks = s // MOBA_BLOCK
    return pl.pallas_call(
        functools.partial(_moba_kernel, n_blocks=n_blocks),
        grid=(b, n_steps, n_blocks),
        in_specs=[
            pl.BlockSpec((1, ATT_BLK, gw), lambda bi, g, i: (bi, i, g)),
            pl.BlockSpec((1, s, gw), lambda bi, g, i: (bi, 0, n_steps + g)),
            pl.BlockSpec((1, n_blocks, gw, ATT_BLK), lambda bi, g, i: (bi, 0, g, 0)),
            pl.BlockSpec((s, LANES), lambda bi, g, i: (0, 0)),
        ],
        out_specs=pl.BlockSpec((1, ATT_BLK, gw), lambda bi, g, i: (bi, i, g)),
        out_shape=jax.ShapeDtypeStruct((b, s, d_model), BF16),
        scratch_shapes=[pltpu.VMEM((ATT_GROUPS, n_blocks, LANES), F32)] + _attn_scratch(),
        compiler_params=_cparams(("parallel", "parallel", "arbitrary")),
        name="moba_attn",
    )(qk, qk, vt, onehot)


def _diff_kernel(q_ref, k_ref, vt_ref, lam_ref, g_ref, o_ref, knm_sc, m_sc, l_sc, acc_sc, *, lambda_init, n_blocks):
    i = pl.program_id(2)
    tq = ATT_BLK

    @pl.when(i == 0)
    def _():
        _store_key_norms(k_ref, knm_sc, n_blocks)

    q_rows = [_split_heads(q_ref[0, :, g * LANES:(g + 1) * LANES])
              for g in range(ATT_GROUPS)]
    o_t = _attend(q_rows, q_rows, k_ref, vt_ref, None, knm_sc, i, m_sc, l_sc, acc_sc)

    lp = lam_ref[...].astype(F32)
    lam = (jnp.exp(jnp.sum(lp[0:1] * lp[1:2], axis=1, keepdims=True))
           - jnp.exp(jnp.sum(lp[2:3] * lp[3:4], axis=1, keepdims=True)) + lambda_init)
    for g in range(ATT_GROUPS):
        o = (o_t[g][:, :tq] - lam * o_t[g][:, tq:]).T
        o = o * lax.rsqrt(jnp.mean(o * o, axis=1, keepdims=True) + SUBLN_EPS)
        o_ref[0, :, g * LANES:(g + 1) * LANES] = (o * g_ref[...] * (1.0 - lambda_init)).astype(o_ref.dtype)


def _diff_attention(q, k, vt, lam_params, subln_g, lambda_init, d_model):
    b, s, _ = q.shape
    gw = ATT_GROUPS * LANES
    n_steps = d_model // gw
    n_blocks = s // ATT_BLK
    return pl.pallas_call(
        functools.partial(_diff_kernel, lambda_init=lambda_init, n_blocks=n_blocks),
        grid=(b, n_steps, n_blocks),
        in_specs=[
            pl.BlockSpec((1, ATT_BLK, gw), lambda bi, h, i: (bi, i, h)),
            pl.BlockSpec((1, s, gw), lambda bi, h, i: (bi, 0, h)),
            pl.BlockSpec((1, n_blocks, gw, ATT_BLK), lambda bi, h, i: (bi, 0, h, 0)),
            pl.BlockSpec(lam_params.shape, lambda bi, h, i: (0, 0)),
            pl.BlockSpec((1, LANES), lambda bi, h, i: (0, 0)),
        ],
        out_specs=pl.BlockSpec((1, ATT_BLK, gw), lambda bi, h, i: (bi, i, h)),
        out_shape=jax.ShapeDtypeStruct((b, s, d_model), BF16),
        scratch_shapes=_attn_scratch(),
        compiler_params=_cparams(("parallel", "parallel", "arbitrary")),
        name="diff_attn",
    )(q, k, vt, lam_params, subln_g)


def _layer_norm(z, g, b):
    mu = jnp.mean(z, axis=1, keepdims=True)
    zc = z - mu
    var = jnp.mean(zc * zc, axis=1, keepdims=True)
    return zc * lax.rsqrt(var + LN_EPS) * g + b


def _oproj_ln_kernel(x_ref, a_ref, w_ref, g_ref, b_ref, o_ref, *, alpha):
    y = jnp.dot(a_ref[...], w_ref[...], preferred_element_type=F32)
    o_ref[...] = _layer_norm(alpha * x_ref[...] + y, g_ref[...], b_ref[...])


def _oproj_ln(x2d, a2d, w_all, layer, g, b, alpha):
    t, d = x2d.shape
    tm = PROJ_TM
    return pl.pallas_call(
        functools.partial(_oproj_ln_kernel, alpha=alpha),
        grid=(t // tm,),
        in_specs=[
            pl.BlockSpec((tm, d), lambda i: (i, 0)),
            pl.BlockSpec((tm, d), lambda i: (i, 0)),
            pl.BlockSpec((None, d, d), lambda i: (layer, 0, 0)),
            pl.BlockSpec((1, d), lambda i: (0, 0)),
            pl.BlockSpec((1, d), lambda i: (0, 0)),
        ],
        out_specs=pl.BlockSpec((tm, d), lambda i: (i, 0)),
        out_shape=jax.ShapeDtypeStruct((t, d), F32),
        compiler_params=_cparams(("parallel",)),
        name="oproj_ln",
    )(x2d, a2d, w_all, g, b)


def _gelu_tanh(x):
    return 0.5 * x * (1.0 + jnp.tanh(math.sqrt(2.0 / math.pi) * (x + 0.044715 * (x * x * x))))


def _ffn_kernel(x_ref, halo_ref, wu_ref, wg_ref, cw_ref, cb_ref, wo_ref, g_ref, b_ref, o_ref, acc_sc,
                *, alpha, tiles_per_seq):
    i = pl.program_id(0)
    j = pl.program_id(1)
    tm = x_ref.shape[0]
    x = x_ref[...]
    xb = x.astype(BF16)
    seq_start = (i % tiles_per_seq) == 0
    halo = jnp.where(seq_start, 0.0, halo_ref[...]).astype(BF16)
    x_ext = jnp.concatenate([halo, xb], axis=0)
    up = jnp.dot(x_ext, wu_ref[...], preferred_element_type=F32)
    gate = jnp.dot(xb, wg_ref[...], preferred_element_type=F32)
    cw = cw_ref[...]
    conv = (up[FFN_HALO - 2:FFN_HALO - 2 + tm] * cw[0:1]
            + up[FFN_HALO - 1:FFN_HALO - 1 + tm] * cw[1:2]
            + up[FFN_HALO:FFN_HALO + tm] * cw[2:3]
            + cb_ref[...])
    h = (_gelu_tanh(conv) * gate).astype(BF16)
    part = jnp.dot(h, wo_ref[...], preferred_element_type=F32)

    @pl.when(j == 0)
    def _():
        acc_sc[...] = part

    @pl.when(j > 0)
    def _():
        acc_sc[...] += part

    @pl.when(j == pl.num_programs(1) - 1)
    def _():
        o_ref[...] = _layer_norm(alpha * x + acc_sc[...], g_ref[...], b_ref[...])


def _ffn_tile(d_ff):
    n = d_ff // LANES
    for parts in range(2, n + 1):
        if n % parts == 0:
            return d_ff // parts
    return d_ff


def _conv_ffn_ln(x2d, w_in_all, conv_w, conv_b, w_out_all, layer, g, b, alpha, seq):
    t, d = x2d.shape
    d_ff = w_out_all.shape[1]
    tm = FFN_TM
    tf = _ffn_tile(d_ff)
    nf = d_ff // tf
    halo_blocks = tm // FFN_HALO
    return pl.pallas_call(
        functools.partial(_ffn_kernel, alpha=alpha, tiles_per_seq=seq // tm),
        grid=(t // tm, nf),
        in_specs=[
            pl.BlockSpec((tm, d), lambda i, j: (i, 0)),
            pl.BlockSpec((FFN_HALO, d), lambda i, j: (jnp.maximum(i * halo_blocks - 1, 0), 0)),
            pl.BlockSpec((None, d, tf), lambda i, j: (layer, 0, j)),
            pl.BlockSpec((None, d, tf), lambda i, j: (layer, 0, nf + j)),
            pl.BlockSpec((CONV_WIDTH, tf), lambda i, j: (0, j)),
            pl.BlockSpec((1, tf), lambda i, j: (0, j)),
            pl.BlockSpec((None, tf, d), lambda i, j: (layer, j, 0)),
            pl.BlockSpec((1, d), lambda i, j: (0, 0)),
            pl.BlockSpec((1, d), lambda i, j: (0, 0)),
        ],
        out_specs=pl.BlockSpec((tm, d), lambda i, j: (i, 0)),
        out_shape=jax.ShapeDtypeStruct((t, d), F32),
        scratch_shapes=[pltpu.VMEM((tm, d), F32)],
        compiler_params=_cparams(("parallel", "arbitrary")),
        name="conv_ffn_ln",
    )(x2d, x2d, w_in_all, w_in_all, conv_w, conv_b, w_out_all, g, b)


def _rotary_tables(positions):
    inv_freq = 1.0 / (ROPE_THETA ** (jnp.arange(ROPE_HALF, dtype=F32) * 2.0 / ROPE_ROT))
    ang = positions.astype(F32).reshape(-1, 1) * inv_freq
    cos, sin = jnp.cos(ang), jnp.sin(ang)
    t = ang.shape[0]
    ones = jnp.ones((t, HEAD_DIM - ROPE_ROT), F32)
    zeros = jnp.zeros((t, HEAD_DIM - ROPE_ROT), F32)
    c_head = jnp.concatenate([cos, cos, ones], axis=1)
    s_head = jnp.concatenate([-sin, sin, zeros], axis=1)
    reps = LANES // HEAD_DIM
    return jnp.tile(c_head, (1, reps)), jnp.tile(s_head, (1, reps))


def kernel(x, positions, a_w_qkv, a_w_o, w_kv_shared, b_w_q, b_w_o, b_lambda, b_subln_g, ln_g, ln_b,
           ffn_w_in, ffn_conv_w, ffn_conv_b, ffn_w_out):
    bsz, seq, d = x.shape
    depth = ffn_w_in.shape[0]
    n_a = a_w_qkv.shape[0]
    t = bsz * seq
    alpha = (2 * depth) ** 0.25
    assert d % (ATT_GROUPS * LANES) == 0
    assert seq % max(PROJ_TM, FFN_TM, ATT_CHUNK * ATT_BLK, PROJ_T_BLOCKS * ATT_BLK) == 0

    cos_tab, sin_tab = _rotary_tables(positions)
    onehot = (jnp.arange(seq)[:, None] // ATT_BLK == jnp.arange(LANES)[None, :]).astype(BF16)
    a_w_qkv_b, a_w_o_b = a_w_qkv.astype(BF16), a_w_o.astype(BF16)
    a_wv_t = jnp.swapaxes(a_w_qkv[:, :, 2 * d:], 1, 2).astype(BF16)
    w_kv_b = w_kv_shared.astype(BF16)[None]
    w_v_sh_t = w_kv_shared[:, d:].T.astype(BF16)[None]
    b_w_q_b, b_w_o_b = b_w_q.astype(BF16), b_w_o.astype(BF16)
    ffn_w_in_b, ffn_w_out_b = ffn_w_in.astype(BF16), ffn_w_out.astype(BF16)

    xs = x.reshape(t, d)
    k_sh = vt_sh = None
    for layer in range(depth):
        g0, b0 = ln_g[layer, 0].reshape(1, d), ln_b[layer, 0].reshape(1, d)
        g1, b1 = ln_g[layer, 1].reshape(1, d), ln_b[layer, 1].reshape(1, d)
        if layer < n_a:
            qk = _proj_rot(xs, a_w_qkv_b, layer, 2 * d, cos_tab, sin_tab, tn=d, n_scaled_tiles=1)
            vt = _proj_t(xs.reshape(bsz, seq, d), a_wv_t, layer)
            o = _moba_attention(qk.reshape(bsz, seq, 2 * d), vt, onehot, d)
            xs = _oproj_ln(xs, o.reshape(t, d), a_w_o_b, layer, g0, b0, alpha)
        else:
            jb = layer - n_a
            if jb == 0:
                k_sh = _proj_rot(xs, w_kv_b, 0, d, cos_tab, sin_tab, tn=d, n_scaled_tiles=0).reshape(bsz, seq, d)
                vt_sh = _proj_t(xs.reshape(bsz, seq, d), w_v_sh_t, 0)
            lambda_init = 0.8 - 0.6 * math.exp(-0.3 * layer)
            q = _proj_rot(xs, b_w_q_b, jb, d, cos_tab, sin_tab, tn=d, n_scaled_tiles=1)
            o = _diff_attention(q.reshape(bsz, seq, d), k_sh, vt_sh, b_lambda[jb],
                                b_subln_g[jb].reshape(1, LANES), lambda_init, d)
            xs = _oproj_ln(xs, o.reshape(t, d), b_w_o_b, jb, g0, b0, alpha)
        xs = _conv_ffn_ln(xs, ffn_w_in_b, ffn_conv_w[layer], ffn_conv_b[layer].reshape(1, -1),
                          ffn_w_out_b, layer, g1, b1, alpha, seq)
    return xs.reshape(bsz, seq, d)
```

```python
import functools
import math

import jax
import jax.numpy as jnp
from jax import lax
from jax.experimental import pallas as pl
from jax.experimental.pallas import tpu as pltpu

F32 = jnp.float32
BF16 = jnp.bfloat16

HEAD_DIM = 64
LANES = 128
SUBLANES = 8
MOBA_BLOCK = 256
MOBA_TOPK = 3
ROPE_THETA = 500000.0
ROPE_ROT = HEAD_DIM // 4
ROPE_HALF = ROPE_ROT // 2
LN_EPS = 1e-5
SUBLN_EPS = 1e-5
CONV_WIDTH = 3
NEG = -1e30

VMEM_LIMIT = 56 * 1024 * 1024
PROJ_TM = 1024
PROJ_PIECE = 512
PROJ_T_BLOCKS = 4
FFN_TM = 512
FFN_HALO = 16
ATT_BLK = MOBA_BLOCK
ATT_CHUNK = 4
ATT_GROUPS = 4
Q_SCALE = HEAD_DIM ** -0.5 * math.log2(math.e)
BOUND_SLACK = 1.05
FAST_BOUND = 32.0

_NT = (((1,), (1,)), ((), ()))


def _cparams(sem):
    return pltpu.CompilerParams(dimension_semantics=sem, vmem_limit_bytes=VMEM_LIMIT)


def _proj_rot_kernel(x_ref, w_ref, c_ref, s_ref, o_ref, *, n_scaled):
    xb = x_ref[...].astype(BF16)
    tm, n = o_ref.shape
    c = c_ref[...]
    s = s_ref[...]
    lane = lax.broadcasted_iota(jnp.int32, (tm, LANES), 1)
    first_half = (lane & (HEAD_DIM - 1)) < ROPE_HALF
    for c0 in range(0, n, PROJ_PIECE):
        y = jnp.dot(xb, w_ref[:, c0:c0 + PROJ_PIECE], preferred_element_type=F32)
        if c0 < n_scaled:
            y = y * Q_SCALE
        cols = []
        for g in range(PROJ_PIECE // LANES):
            yg = y[:, g * LANES:(g + 1) * LANES]
            partner = jnp.where(first_half, pltpu.roll(yg, LANES - ROPE_HALF, 1), pltpu.roll(yg, ROPE_HALF, 1))
            cols.append(yg * c + partner * s)
        o_ref[:, c0:c0 + PROJ_PIECE] = jnp.concatenate(cols, axis=1).astype(o_ref.dtype)


def _proj_rot(x2d, w_all, layer, n, cos_tab, sin_tab, n_scaled):
    t, k = x2d.shape
    tm = PROJ_TM
    assert n % PROJ_PIECE == 0 and n_scaled % PROJ_PIECE == 0
    return pl.pallas_call(
        functools.partial(_proj_rot_kernel, n_scaled=n_scaled),
        grid=(t // tm,),
        in_specs=[
            pl.BlockSpec((tm, k), lambda i: (i, 0)),
            pl.BlockSpec((None, k, n), lambda i: (layer, 0, 0), pipeline_mode=pl.Buffered(1)),
            pl.BlockSpec((tm, LANES), lambda i: (i, 0)),
            pl.BlockSpec((tm, LANES), lambda i: (i, 0)),
        ],
        out_specs=pl.BlockSpec((tm, n), lambda i: (i, 0)),
        out_shape=jax.ShapeDtypeStruct((t, n), BF16),
        compiler_params=_cparams(("parallel",)),
        name="proj_rot",
    )(x2d, w_all, cos_tab, sin_tab)


def _proj_t_kernel(x_ref, wt_ref, o_ref):
    xb = x_ref[0].astype(BF16)
    y = lax.dot_general(wt_ref[...], xb, _NT, preferred_element_type=F32)
    for j in range(PROJ_T_BLOCKS):
        o_ref[0, j] = y[:, j * ATT_BLK:(j + 1) * ATT_BLK].astype(o_ref.dtype)


def _proj_t(x3d, wt_all, layer):
    b, s, d = x3d.shape
    n = wt_all.shape[1]
    rows = PROJ_T_BLOCKS * ATT_BLK
    return pl.pallas_call(
        _proj_t_kernel,
        grid=(b, s // rows),
        in_specs=[
            pl.BlockSpec((1, rows, d), lambda bi, j: (bi, j, 0)),
            pl.BlockSpec((None, n, d), lambda bi, j: (layer, 0, 0)),
        ],
        out_specs=pl.BlockSpec((1, PROJ_T_BLOCKS, n, ATT_BLK), lambda bi, j: (bi, j, 0, 0)),
        out_shape=jax.ShapeDtypeStruct((b, s // ATT_BLK, n, ATT_BLK), BF16),
        compiler_params=_cparams(("parallel", "parallel")),
        name="proj_t",
    )(x3d, wt_all)


def _split_heads(q2):
    lane = lax.broadcasted_iota(jnp.int32, q2.shape, 1)
    zero = jnp.zeros_like(q2)
    q_a = jnp.where(lane < HEAD_DIM, q2, zero)
    q_b = jnp.where(lane >= HEAD_DIM, q2, zero)
    return jnp.concatenate([q_a, q_b], axis=0)


def _head_selector():
    r = lax.broadcasted_iota(jnp.int32, (SUBLANES, LANES), 0)
    lane = lax.broadcasted_iota(jnp.int32, (SUBLANES, LANES), 1)
    return jnp.where(((r == 0) & (lane < HEAD_DIM)) | ((r == 1) & (lane >= HEAD_DIM)), 1.0, 0.0).astype(BF16)


def _sq_norms_t(sel, x):
    xf = x.astype(F32)
    return lax.dot_general(sel, (xf * xf).astype(BF16), _NT, preferred_element_type=F32)


def _store_key_norms(k_ref, knm_sc, n_blocks):
    sel = _head_selector()
    for g in range(ATT_GROUPS):
        kn2 = jnp.zeros((SUBLANES, 1), F32)
        for j in range(n_blocks):
            kj = k_ref[0, j * ATT_BLK:(j + 1) * ATT_BLK, g * LANES:(g + 1) * LANES]
            kn2 = jnp.maximum(kn2, jnp.max(_sq_norms_t(sel, kj), axis=1, keepdims=True))
        knm_sc[g] = jnp.broadcast_to(kn2, (SUBLANES, LANES))


def _attend(q_rows, q_aug, k_ref, vt_ref, e_ref, knm_sc, i, m_sc, l_sc, acc_sc):
    tk = ATT_CHUNK * ATT_BLK
    groups = range(len(q_aug))
    n_cols = 2 * ATT_BLK

    def scores(g, start, size):
        k_c = k_ref[0, pl.ds(start, size), g * LANES:(g + 1) * LANES]
        if e_ref is not None:
            k_c = jnp.concatenate([k_c, e_ref[pl.ds(start, size), :]], axis=1)
        return lax.dot_general(k_c, q_aug[g], _NT, preferred_element_type=F32)

    def vt_cols(g, blk0, n):
        return jnp.concatenate([vt_ref[0, blk0 + b, g * LANES:(g + 1) * LANES, :] for b in range(n)], axis=1)

    def causal(s_own):
        key_pos = lax.broadcasted_iota(jnp.int32, s_own.shape, 0)
        q_pos = lax.broadcasted_iota(jnp.int32, s_own.shape, 1) & (ATT_BLK - 1)
        return jnp.where(key_pos <= q_pos, s_own, NEG)

    def sublane_sums(p):
        return jnp.sum(p.reshape(-1, SUBLANES, n_cols), axis=0)

    ones_row = (lax.broadcasted_iota(jnp.int32, (SUBLANES, LANES), 0) == 0).astype(BF16)
    col = lax.broadcasted_iota(jnp.int32, (1, n_cols), 1)
    bound = []
    for g in groups:
        qn2 = _sq_norms_t(ones_row, q_rows[g])[0:1]
        kn2 = jnp.where(col < ATT_BLK, knm_sc[g, 0:1, 0:1], knm_sc[g, 1:2, 0:1])
        bound.append(jnp.sqrt(qn2 * kn2) * BOUND_SLACK)
    worst = functools.reduce(jnp.maximum, [jnp.max(b) for b in bound])
    fast = worst <= FAST_BOUND

    def fast_tail(n_tail):
        blk0 = i - n_tail
        start = pl.multiple_of(blk0 * ATT_BLK, ATT_BLK)
        for g in groups:
            s_t = scores(g, start, (n_tail + 1) * ATT_BLK)
            s_own = causal(s_t[n_tail * ATT_BLK:])
            s_t = jnp.concatenate([s_t[:n_tail * ATT_BLK], s_own], axis=0) if n_tail else s_own
            p = jnp.exp2(s_t - bound[g])
            l_sc[g] = sublane_sums(p)
            acc_sc[g] = jnp.dot(vt_cols(g, blk0, n_tail + 1), p.astype(BF16), preferred_element_type=F32)

    def fast_body(c, carry):
        start = pl.multiple_of(c * tk, tk)
        for g in groups:
            p = jnp.exp2(scores(g, start, tk) - bound[g])
            l_sc[g] += sublane_sums(p)
            acc_sc[g] += jnp.dot(vt_cols(g, c * ATT_CHUNK, ATT_CHUNK), p.astype(BF16),
                                 preferred_element_type=F32)
        return carry

    @pl.when(fast)
    def _():
        for n_tail in range(ATT_CHUNK):
            pl.when((i & (ATT_CHUNK - 1)) == n_tail)(functools.partial(fast_tail, n_tail))
        lax.fori_loop(0, i // ATT_CHUNK, fast_body, 0)
        for g in groups:
            acc_sc[g] = acc_sc[g] / jnp.sum(l_sc[g], axis=0, keepdims=True)

    @pl.when(jnp.logical_not(fast))
    def _():
        own = pl.multiple_of(i * ATT_BLK, ATT_BLK)
        for g in groups:
            s_t = causal(scores(g, own, ATT_BLK))
            m = jnp.max(s_t, axis=0, keepdims=True)
            p = jnp.exp2(s_t - m)
            m_sc[g] = m
            l_sc[g, 0:1, :] = jnp.sum(p, axis=0, keepdims=True)
            acc_sc[g] = jnp.dot(vt_cols(g, i, 1), p.astype(BF16), preferred_element_type=F32)

        def exact_body(j, carry):
            start = pl.multiple_of(j * ATT_BLK, ATT_BLK)
            for g in groups:
                s_j = scores(g, start, ATT_BLK)
                m_old = m_sc[g]
                m_new = jnp.maximum(m_old, jnp.max(s_j, axis=0, keepdims=True))
                alpha = jnp.exp2(m_old - m_new)
                p = jnp.exp2(s_j - m_new)
                l_sc[g, 0:1, :] = alpha * l_sc[g, 0:1, :] + jnp.sum(p, axis=0, keepdims=True)
                acc_sc[g] = alpha * acc_sc[g] + jnp.dot(vt_cols(g, j, 1), p.astype(BF16),
                                                        preferred_element_type=F32)
                m_sc[g] = m_new
            return carry

        lax.fori_loop(0, i, exact_body, 0)
        for g in groups:
            acc_sc[g] = acc_sc[g] / l_sc[g, 0:1, :]

    return [acc_sc[g] for g in groups]


def _attn_scratch():
    return [
        pltpu.VMEM((ATT_GROUPS, SUBLANES, LANES), F32),
        pltpu.VMEM((ATT_GROUPS, 1, 2 * ATT_BLK), F32),
        pltpu.VMEM((ATT_GROUPS, SUBLANES, 2 * ATT_BLK), F32),
        pltpu.VMEM((ATT_GROUPS, LANES, 2 * ATT_BLK), F32),
    ]


def _moba_kernel(q_ref, k_ref, vt_ref, e_ref, o_ref, kmean_sc, knm_sc, m_sc, l_sc, acc_sc, *, n_blocks):
    i = pl.program_id(2)
    tq = ATT_BLK

    @pl.when(i == 0)
    def _():
        _store_key_norms(k_ref, knm_sc, n_blocks)
        for g in range(ATT_GROUPS):
            for j in range(n_blocks):
                kj = k_ref[0, j * MOBA_BLOCK:(j + 1) * MOBA_BLOCK, g * LANES:(g + 1) * LANES].astype(F32)
                kmean_sc[g, j:j + 1, :] = jnp.mean(kj, axis=0, keepdims=True)

    q_rows, q_aug = [], []
    for g in range(ATT_GROUPS):
        qs = _split_heads(q_ref[0, :, g * LANES:(g + 1) * LANES])
        q_rows.append(qs)
        gate_t = lax.dot_general(kmean_sc[g].astype(BF16), qs, _NT, preferred_element_type=F32)
        blk = lax.broadcasted_iota(jnp.int32, gate_t.shape, 0)
        cnt = jnp.zeros(gate_t.shape, F32)
        for jp in range(n_blocks):
            row = gate_t[jp:jp + 1, :]
            beats = (row > gate_t) | ((row == gate_t) & (jp < blk))
            cnt = cnt + jnp.where(beats, jnp.where(jp < i, 1.0, 0.0), 0.0)
        selected = (cnt < MOBA_TOPK) & (blk < i)
        bias_t = jnp.where(selected | (blk == i), 0.0, NEG)
        bias_pad = jnp.concatenate([bias_t, jnp.full((LANES - n_blocks, 2 * tq), NEG, F32)], axis=0)
        q_aug.append(jnp.concatenate([qs, bias_pad.T.astype(BF16)], axis=1))

    o_t = _attend(q_rows, q_aug, k_ref, vt_ref, e_ref, knm_sc, i, m_sc, l_sc, acc_sc)
    for g in range(ATT_GROUPS):
        o_sel = jnp.concatenate([o_t[g][:HEAD_DIM, :tq], o_t[g][HEAD_DIM:, tq:]], axis=0)
        o_ref[0, :, g * LANES:(g + 1) * LANES] = o_sel.T.astype(o_ref.dtype)


def _moba_attention(qk, vt, onehot, d_model):
    b, s, _ = qk.shape
    gw = ATT_GROUPS * LANES
    n_steps = d_model // gw
    n_blocks = s // MOBA_BLOCK
    return pl.pallas_call(
        functools.partial(_moba_kernel, n_blocks=n_blocks),
        grid=(b, n_steps, n_blocks),
        in_specs=[
            pl.BlockSpec((1, ATT_BLK, gw), lambda bi, g, i: (bi, i, g)),
            pl.BlockSpec((1, s, gw), lambda bi, g, i: (bi, 0, n_steps + g)),
            pl.BlockSpec((1, n_blocks, gw, ATT_BLK), lambda bi, g, i: (bi, 0, g, 0)),
            pl.BlockSpec((s, LANES), lambda bi, g, i: (0, 0)),
        ],
        out_specs=pl.BlockSpec((1, ATT_BLK, gw), lambda bi, g, i: (bi, i, g)),
        out_shape=jax.ShapeDtypeStruct((b, s, d_model), BF16),
        scratch_shapes=[pltpu.VMEM((ATT_GROUPS, n_blocks, LANES), F32)] + _attn_scratch(),
        compiler_params=_cparams(("parallel", "parallel", "arbitrary")),
        name="moba_attn",
    )(qk, qk, vt, onehot)


def _diff_kernel(q_ref, k_ref, vt_ref, lam_ref, g_ref, o_ref, knm_sc, m_sc, l_sc, acc_sc, *, lambda_init, n_blocks):
    i = pl.program_id(2)
    tq = ATT_BLK

    @pl.when(i == 0)
    def _():
        _store_key_norms(k_ref, knm_sc, n_blocks)

    q_rows = [_split_heads(q_ref[0, :, g * LANES:(g + 1) * LANES])
              for g in range(ATT_GROUPS)]
    o_t = _attend(q_rows, q_rows, k_ref, vt_ref, None, knm_sc, i, m_sc, l_sc, acc_sc)

    lp = lam_ref[...].astype(F32)
    lam = (jnp.exp(jnp.sum(lp[0:1] * lp[1:2], axis=1, keepdims=True))
           - jnp.exp(jnp.sum(lp[2:3] * lp[3:4], axis=1, keepdims=True)) + lambda_init)
    for g in range(ATT_GROUPS):
        o = (o_t[g][:, :tq] - lam * o_t[g][:, tq:]).T
        o = o * lax.rsqrt(jnp.mean(o * o, axis=1, keepdims=True) + SUBLN_EPS)
        o_ref[0, :, g * LANES:(g + 1) * LANES] = (o * g_ref[...] * (1.0 - lambda_init)).astype(o_ref.dtype)


def _diff_attention(q, k, vt, lam_params, subln_g, lambda_init, d_model):
    b, s, _ = q.shape
    gw = ATT_GROUPS * LANES
    n_steps = d_model // gw
    n_blocks = s // ATT_BLK
    return pl.pallas_call(
        functools.partial(_diff_kernel, lambda_init=lambda_init, n_blocks=n_blocks),
        grid=(b, n_steps, n_blocks),
        in_specs=[
            pl.BlockSpec((1, ATT_BLK, gw), lambda bi, h, i: (bi, i, h)),
            pl.BlockSpec((1, s, gw), lambda bi, h, i: (bi, 0, h)),
            pl.BlockSpec((1, n_blocks, gw, ATT_BLK), lambda bi, h, i: (bi, 0, h, 0)),
            pl.BlockSpec(lam_params.shape, lambda bi, h, i: (0, 0)),
            pl.BlockSpec((1, LANES), lambda bi, h, i: (0, 0)),
        ],
        out_specs=pl.BlockSpec((1, ATT_BLK, gw), lambda bi, h, i: (bi, i, h)),
        out_shape=jax.ShapeDtypeStruct((b, s, d_model), BF16),
        scratch_shapes=_attn_scratch(),
        compiler_params=_cparams(("parallel", "parallel", "arbitrary")),
        name="diff_attn",
    )(q, k, vt, lam_params, subln_g)


def _layer_norm(z, g, b):
    mu = jnp.mean(z, axis=1, keepdims=True)
    zc = z - mu
    var = jnp.mean(zc * zc, axis=1, keepdims=True)
    return zc * lax.rsqrt(var + LN_EPS) * g + b


def _oproj_ln_kernel(x_ref, a_ref, w_ref, g_ref, b_ref, o_ref, *, alpha):
    y = jnp.dot(a_ref[...], w_ref[...], preferred_element_type=F32)
    o_ref[...] = _layer_norm(alpha * x_ref[...] + y, g_ref[...], b_ref[...])


def _oproj_ln(x2d, a2d, w_all, layer, g, b, alpha):
    t, d = x2d.shape
    tm = PROJ_TM
    return pl.pallas_call(
        functools.partial(_oproj_ln_kernel, alpha=alpha),
        grid=(t // tm,),
        in_specs=[
            pl.BlockSpec((tm, d), lambda i: (i, 0)),
            pl.BlockSpec((tm, d), lambda i: (i, 0)),
            pl.BlockSpec((None, d, d), lambda i: (layer, 0, 0)),
            pl.BlockSpec((1, d), lambda i: (0, 0)),
            pl.BlockSpec((1, d), lambda i: (0, 0)),
        ],
        out_specs=pl.BlockSpec((tm, d), lambda i: (i, 0)),
        out_shape=jax.ShapeDtypeStruct((t, d), F32),
        compiler_params=_cparams(("parallel",)),
        name="oproj_ln",
    )(x2d, a2d, w_all, g, b)


def _gelu_tanh(x):
    return 0.5 * x * (1.0 + jnp.tanh(math.sqrt(2.0 / math.pi) * (x + 0.044715 * (x * x * x))))


def _ffn_kernel(x_ref, halo_ref, wi_ref, cw_ref, cb_ref, wo_ref, g_ref, b_ref, o_ref, up_sc,
                *, alpha, tiles_per_seq, tf):
    i = pl.program_id(0)
    tm = x_ref.shape[0]
    d_ff = wo_ref.shape[0]
    x = x_ref[...]
    xb = x.astype(BF16)
    seq_start = (i % tiles_per_seq) == 0
    halo = jnp.where(seq_start, 0.0, halo_ref[...]).astype(BF16)
    x_ext = jnp.concatenate([halo, xb], axis=0)
    f = None
    for c0 in range(0, d_ff, tf):
        up_sc[...] = jnp.dot(x_ext, wi_ref[:, c0:c0 + tf], preferred_element_type=F32)
        gate = jnp.dot(xb, wi_ref[:, d_ff + c0:d_ff + c0 + tf], preferred_element_type=F32)
        cw = cw_ref[:, c0:c0 + tf]
        conv = (up_sc[FFN_HALO - 2:FFN_HALO - 2 + tm, :] * cw[0:1]
                + up_sc[FFN_HALO - 1:FFN_HALO - 1 + tm, :] * cw[1:2]
                + up_sc[FFN_HALO:FFN_HALO + tm, :] * cw[2:3]
                + cb_ref[:, c0:c0 + tf])
        h = (_gelu_tanh(conv) * gate).astype(BF16)
        part = jnp.dot(h, wo_ref[c0:c0 + tf, :], preferred_element_type=F32)
        f = part if f is None else f + part
    o_ref[...] = _layer_norm(alpha * x + f, g_ref[...], b_ref[...])


def _ffn_tile(d_ff):
    n = d_ff // LANES
    for parts in range(2, n + 1):
        if n % parts == 0:
            return d_ff // parts
    return d_ff


def _conv_ffn_ln(x2d, w_in_all, conv_w, conv_b, w_out_all, layer, g, b, alpha, seq):
    t, d = x2d.shape
    d_ff = w_out_all.shape[1]
    tm = FFN_TM
    tf = _ffn_tile(d_ff)
    halo_blocks = tm // FFN_HALO
    resident = pl.Buffered(1)
    return pl.pallas_call(
        functools.partial(_ffn_kernel, alpha=alpha, tiles_per_seq=seq // tm, tf=tf),
        grid=(t // tm,),
        in_specs=[
            pl.BlockSpec((tm, d), lambda i: (i, 0)),
            pl.BlockSpec((FFN_HALO, d), lambda i: (jnp.maximum(i * halo_blocks - 1, 0), 0)),
            pl.BlockSpec((None, d, 2 * d_ff), lambda i: (layer, 0, 0), pipeline_mode=resident),
            pl.BlockSpec((CONV_WIDTH, d_ff), lambda i: (0, 0)),
            pl.BlockSpec((1, d_ff), lambda i: (0, 0)),
            pl.BlockSpec((None, d_ff, d), lambda i: (layer, 0, 0), pipeline_mode=resident),
            pl.BlockSpec((1, d), lambda i: (0, 0)),
            pl.BlockSpec((1, d), lambda i: (0, 0)),
        ],
        out_specs=pl.BlockSpec((tm, d), lambda i: (i, 0)),
        out_shape=jax.ShapeDtypeStruct((t, d), F32),
        scratch_shapes=[pltpu.VMEM((FFN_HALO + tm, tf), F32)],
        compiler_params=_cparams(("parallel",)),
        name="conv_ffn_ln",
    )(x2d, x2d, w_in_all, conv_w, conv_b, w_out_all, g, b)


def _rotary_tables(positions):
    inv_freq = 1.0 / (ROPE_THETA ** (jnp.arange(ROPE_HALF, dtype=F32) * 2.0 / ROPE_ROT))
    ang = positions.astype(F32).reshape(-1, 1) * inv_freq
    cos, sin = jnp.cos(ang), jnp.sin(ang)
    t = ang.shape[0]
    ones = jnp.ones((t, HEAD_DIM - ROPE_ROT), F32)
    zeros = jnp.zeros((t, HEAD_DIM - ROPE_ROT), F32)
    c_head = jnp.concatenate([cos, cos, ones], axis=1)
    s_head = jnp.concatenate([-sin, sin, zeros], axis=1)
    reps = LANES // HEAD_DIM
    return jnp.tile(c_head, (1, reps)), jnp.tile(s_head, (1, reps))


def kernel(x, positions, a_w_qkv, a_w_o, w_kv_shared, b_w_q, b_w_o, b_lambda, b_subln_g, ln_g, ln_b,
           ffn_w_in, ffn_conv_w, ffn_conv_b, ffn_w_out):
    bsz, seq, d = x.shape
    depth = ffn_w_in.shape[0]
    n_a = a_w_qkv.shape[0]
    t = bsz * seq
    alpha = (2 * depth) ** 0.25
    assert d % (ATT_GROUPS * LANES) == 0
    assert seq % max(PROJ_TM, FFN_TM, ATT_CHUNK * ATT_BLK, PROJ_T_BLOCKS * ATT_BLK) == 0

    cos_tab, sin_tab = _rotary_tables(positions)
    onehot = (jnp.arange(seq)[:, None] // ATT_BLK == jnp.arange(LANES)[None, :]).astype(BF16)
    a_w_qkv_b, a_w_o_b = a_w_qkv.astype(BF16), a_w_o.astype(BF16)
    a_wv_t = jnp.swapaxes(a_w_qkv[:, :, 2 * d:], 1, 2).astype(BF16)
    w_kv_b = w_kv_shared.astype(BF16)[None]
    w_v_sh_t = w_kv_shared[:, d:].T.astype(BF16)[None]
    b_w_q_b, b_w_o_b = b_w_q.astype(BF16), b_w_o.astype(BF16)
    ffn_w_in_b, ffn_w_out_b = ffn_w_in.astype(BF16), ffn_w_out.astype(BF16)

    xs = x.reshape(t, d)
    k_sh = vt_sh = None
    for layer in range(depth):
        g0, b0 = ln_g[layer, 0].reshape(1, d), ln_b[layer, 0].reshape(1, d)
        g1, b1 = ln_g[layer, 1].reshape(1, d), ln_b[layer, 1].reshape(1, d)
        if layer < n_a:
            qk = _proj_rot(xs, a_w_qkv_b, layer, 2 * d, cos_tab, sin_tab, n_scaled=d)
            vt = _proj_t(xs.reshape(bsz, seq, d), a_wv_t, layer)
            o = _moba_attention(qk.reshape(bsz, seq, 2 * d), vt, onehot, d)
            xs = _oproj_ln(xs, o.reshape(t, d), a_w_o_b, layer, g0, b0, alpha)
        else:
            jb = layer - n_a
            if jb == 0:
                k_sh = _proj_rot(xs, w_kv_b, 0, d, cos_tab, sin_tab, n_scaled=0).reshape(bsz, seq, d)
                vt_sh = _proj_t(xs.reshape(bsz, seq, d), w_v_sh_t, 0)
            lambda_init = 0.8 - 0.6 * math.exp(-0.3 * layer)
            q = _proj_rot(xs, b_w_q_b, jb, d, cos_tab, sin_tab, n_scaled=d)
            o = _diff_attention(q.reshape(bsz, seq, d), k_sh, vt_sh, b_lambda[jb],
                                b_subln_g[jb].reshape(1, LANES), lambda_init, d)
            xs = _oproj_ln(xs, o.reshape(t, d), b_w_o_b, jb, g0, b0, alpha)
        xs = _conv_ffn_ln(xs, ffn_w_in_b, ffn_conv_w[layer], ffn_conv_b[layer].reshape(1, -1),
                          ffn_w_out_b, layer, g1, b1, alpha, seq)
    return xs.reshape(bsz, seq, d)
```

```python
import functools
import math

import jax
import jax.numpy as jnp
from jax import lax
from jax.experimental import pallas as pl
from jax.experimental.pallas import tpu as pltpu

F32 = jnp.float32
BF16 = jnp.bfloat16

HEAD_DIM = 64
LANES = 128
SUBLANES = 8
MOBA_BLOCK = 256
MOBA_TOPK = 3
ROPE_THETA = 500000.0
ROPE_ROT = HEAD_DIM // 4
ROPE_HALF = ROPE_ROT // 2
LN_EPS = 1e-5
SUBLN_EPS = 1e-5
CONV_WIDTH = 3
NEG = -1e30

VMEM_LIMIT = 56 * 1024 * 1024
PROJ_TM = 1024
PROJ_PIECE = 512
PROJ_T_BLOCKS = 4
FFN_TM = 512
FFN_HALO = 16
OPROJ_ROW_GROUPS = 4
ATT_BLK = MOBA_BLOCK
ATT_CHUNK = 4
ATT_GROUPS = 4
Q_SCALE = HEAD_DIM ** -0.5 * math.log2(math.e)
BOUND_SLACK = 1.05
FAST_BOUND = 32.0

_NT = (((1,), (1,)), ((), ()))


def _cparams(sem):
    return pltpu.CompilerParams(dimension_semantics=sem, vmem_limit_bytes=VMEM_LIMIT)


def _proj_rot_kernel(x_ref, w_ref, c_ref, s_ref, o_ref, *, n_scaled):
    xb = x_ref[...].astype(BF16)
    tm, n = o_ref.shape
    c = c_ref[...]
    s = s_ref[...]
    lane = lax.broadcasted_iota(jnp.int32, (tm, LANES), 1)
    first_half = (lane & (HEAD_DIM - 1)) < ROPE_HALF
    for c0 in range(0, n, PROJ_PIECE):
        y = jnp.dot(xb, w_ref[:, c0:c0 + PROJ_PIECE], preferred_element_type=F32)
        if c0 < n_scaled:
            y = y * Q_SCALE
        cols = []
        for g in range(PROJ_PIECE // LANES):
            yg = y[:, g * LANES:(g + 1) * LANES]
            partner = jnp.where(first_half, pltpu.roll(yg, LANES - ROPE_HALF, 1), pltpu.roll(yg, ROPE_HALF, 1))
            cols.append(yg * c + partner * s)
        o_ref[:, c0:c0 + PROJ_PIECE] = jnp.concatenate(cols, axis=1).astype(o_ref.dtype)


def _proj_rot(x2d, w_all, layer, n, cos_tab, sin_tab, n_scaled):
    t, k = x2d.shape
    tm = PROJ_TM
    assert n % PROJ_PIECE == 0 and n_scaled % PROJ_PIECE == 0
    return pl.pallas_call(
        functools.partial(_proj_rot_kernel, n_scaled=n_scaled),
        grid=(t // tm,),
        in_specs=[
            pl.BlockSpec((tm, k), lambda i: (i, 0)),
            pl.BlockSpec((None, k, n), lambda i: (layer, 0, 0), pipeline_mode=pl.Buffered(1)),
            pl.BlockSpec((tm, LANES), lambda i: (i, 0)),
            pl.BlockSpec((tm, LANES), lambda i: (i, 0)),
        ],
        out_specs=pl.BlockSpec((tm, n), lambda i: (i, 0)),
        out_shape=jax.ShapeDtypeStruct((t, n), BF16),
        compiler_params=_cparams(("parallel",)),
        name="proj_rot",
    )(x2d, w_all, cos_tab, sin_tab)


def _proj_t_kernel(x_ref, wt_ref, o_ref):
    xb = x_ref[0].astype(BF16)
    y = lax.dot_general(wt_ref[...], xb, _NT, preferred_element_type=F32)
    for j in range(PROJ_T_BLOCKS):
        o_ref[0, j] = y[:, j * ATT_BLK:(j + 1) * ATT_BLK].astype(o_ref.dtype)


def _proj_t(x3d, wt_all, layer):
    b, s, d = x3d.shape
    n = wt_all.shape[1]
    rows = PROJ_T_BLOCKS * ATT_BLK
    return pl.pallas_call(
        _proj_t_kernel,
        grid=(b, s // rows),
        in_specs=[
            pl.BlockSpec((1, rows, d), lambda bi, j: (bi, j, 0)),
            pl.BlockSpec((None, n, d), lambda bi, j: (layer, 0, 0)),
        ],
        out_specs=pl.BlockSpec((1, PROJ_T_BLOCKS, n, ATT_BLK), lambda bi, j: (bi, j, 0, 0)),
        out_shape=jax.ShapeDtypeStruct((b, s // ATT_BLK, n, ATT_BLK), BF16),
        compiler_params=_cparams(("parallel", "parallel")),
        name="proj_t",
    )(x3d, wt_all)


def _split_heads(q2):
    lane = lax.broadcasted_iota(jnp.int32, q2.shape, 1)
    zero = jnp.zeros_like(q2)
    q_a = jnp.where(lane < HEAD_DIM, q2, zero)
    q_b = jnp.where(lane >= HEAD_DIM, q2, zero)
    return jnp.concatenate([q_a, q_b], axis=0)


def _head_selector():
    r = lax.broadcasted_iota(jnp.int32, (SUBLANES, LANES), 0)
    lane = lax.broadcasted_iota(jnp.int32, (SUBLANES, LANES), 1)
    return jnp.where(((r == 0) & (lane < HEAD_DIM)) | ((r == 1) & (lane >= HEAD_DIM)), 1.0, 0.0).astype(BF16)


def _sq_norms_t(sel, x):
    xf = x.astype(F32)
    return lax.dot_general(sel, (xf * xf).astype(BF16), _NT, preferred_element_type=F32)


def _store_key_norms(k_ref, knm_sc, n_blocks):
    sel = _head_selector()
    for g in range(ATT_GROUPS):
        kn2 = jnp.zeros((SUBLANES, 1), F32)
        for j in range(n_blocks):
            kj = k_ref[0, j * ATT_BLK:(j + 1) * ATT_BLK, g * LANES:(g + 1) * LANES]
            kn2 = jnp.maximum(kn2, jnp.max(_sq_norms_t(sel, kj), axis=1, keepdims=True))
        knm_sc[g] = jnp.broadcast_to(kn2, (SUBLANES, LANES))


def _attend(q_rows, k_ref, vt_ref, bias_sc, knm_sc, i, m_sc, l_sc, acc_sc):
    tk = ATT_CHUNK * ATT_BLK
    groups = range(len(q_rows))
    n_cols = 2 * ATT_BLK

    def scores(g, start, size):
        k_c = k_ref[0, pl.ds(start, size), g * LANES:(g + 1) * LANES]
        return lax.dot_general(k_c, q_rows[g], _NT, preferred_element_type=F32)

    def add_rows(g, s_t, blk0, n, neg_shift):
        if bias_sc is None:
            return s_t if neg_shift is None else s_t - neg_shift
        s3 = s_t.reshape(n, ATT_BLK, n_cols) + bias_sc[g, pl.ds(blk0, n)]
        return s3.reshape(n * ATT_BLK, n_cols)

    def vt_cols(g, blk0, n):
        return jnp.concatenate([vt_ref[0, blk0 + b, g * LANES:(g + 1) * LANES, :] for b in range(n)], axis=1)

    def causal(s_own):
        key_pos = lax.broadcasted_iota(jnp.int32, s_own.shape, 0)
        q_pos = lax.broadcasted_iota(jnp.int32, s_own.shape, 1) & (ATT_BLK - 1)
        return jnp.where(key_pos <= q_pos, s_own, NEG)

    def sublane_sums(p):
        return jnp.sum(p.reshape(-1, SUBLANES, n_cols), axis=0)

    ones_row = (lax.broadcasted_iota(jnp.int32, (SUBLANES, LANES), 0) == 0).astype(BF16)
    col = lax.broadcasted_iota(jnp.int32, (1, n_cols), 1)
    bound = []
    for g in groups:
        qn2 = _sq_norms_t(ones_row, q_rows[g])[0:1]
        kn2 = jnp.where(col < ATT_BLK, knm_sc[g, 0:1, 0:1], knm_sc[g, 1:2, 0:1])
        bound.append(jnp.sqrt(qn2 * kn2) * BOUND_SLACK)
    worst = functools.reduce(jnp.maximum, [jnp.max(b) for b in bound])
    fast = worst <= FAST_BOUND

    def fast_tail(n_tail):
        blk0 = i - n_tail
        start = pl.multiple_of(blk0 * ATT_BLK, ATT_BLK)
        for g in groups:
            s_t = scores(g, start, (n_tail + 1) * ATT_BLK)
            s_own = causal(s_t[n_tail * ATT_BLK:])
            s_t = jnp.concatenate([s_t[:n_tail * ATT_BLK], s_own], axis=0) if n_tail else s_own
            p = jnp.exp2(add_rows(g, s_t, blk0, n_tail + 1, bound[g]))
            l_sc[g] = sublane_sums(p)
            acc_sc[g] = jnp.dot(vt_cols(g, blk0, n_tail + 1), p.astype(BF16), preferred_element_type=F32)

    def fast_body(c, carry):
        start = pl.multiple_of(c * tk, tk)
        for g in groups:
            p = jnp.exp2(add_rows(g, scores(g, start, tk), c * ATT_CHUNK, ATT_CHUNK, bound[g]))
            l_sc[g] += sublane_sums(p)
            acc_sc[g] += jnp.dot(vt_cols(g, c * ATT_CHUNK, ATT_CHUNK), p.astype(BF16),
                                 preferred_element_type=F32)
        return carry

    @pl.when(fast)
    def _():
        if bias_sc is not None:
            for g in groups:
                bias_sc[g] = bias_sc[g] - bound[g]
        for n_tail in range(ATT_CHUNK):
            pl.when((i & (ATT_CHUNK - 1)) == n_tail)(functools.partial(fast_tail, n_tail))
        lax.fori_loop(0, i // ATT_CHUNK, fast_body, 0)
        for g in groups:
            acc_sc[g] = acc_sc[g] / jnp.sum(l_sc[g], axis=0, keepdims=True)

    @pl.when(jnp.logical_not(fast))
    def _():
        own = pl.multiple_of(i * ATT_BLK, ATT_BLK)
        for g in groups:
            s_t = causal(scores(g, own, ATT_BLK))
            m = jnp.max(s_t, axis=0, keepdims=True)
            p = jnp.exp2(s_t - m)
            m_sc[g] = m
            l_sc[g, 0:1, :] = jnp.sum(p, axis=0, keepdims=True)
            acc_sc[g] = jnp.dot(vt_cols(g, i, 1), p.astype(BF16), preferred_element_type=F32)

        def exact_body(j, carry):
            start = pl.multiple_of(j * ATT_BLK, ATT_BLK)
            for g in groups:
                s_j = add_rows(g, scores(g, start, ATT_BLK), j, 1, None)
                m_old = m_sc[g]
                m_new = jnp.maximum(m_old, jnp.max(s_j, axis=0, keepdims=True))
                alpha = jnp.exp2(m_old - m_new)
                p = jnp.exp2(s_j - m_new)
                l_sc[g, 0:1, :] = alpha * l_sc[g, 0:1, :] + jnp.sum(p, axis=0, keepdims=True)
                acc_sc[g] = alpha * acc_sc[g] + jnp.dot(vt_cols(g, j, 1), p.astype(BF16),
                                                        preferred_element_type=F32)
                m_sc[g] = m_new
            return carry

        lax.fori_loop(0, i, exact_body, 0)
        for g in groups:
            acc_sc[g] = acc_sc[g] / l_sc[g, 0:1, :]

    return [acc_sc[g] for g in groups]


def _attn_scratch():
    return [
        pltpu.VMEM((ATT_GROUPS, SUBLANES, LANES), F32),
        pltpu.VMEM((ATT_GROUPS, 1, 2 * ATT_BLK), F32),
        pltpu.VMEM((ATT_GROUPS, SUBLANES, 2 * ATT_BLK), F32),
        pltpu.VMEM((ATT_GROUPS, LANES, 2 * ATT_BLK), F32),
    ]


def _moba_kernel(q_ref, k_ref, vt_ref, o_ref, kmean_sc, bias_sc, knm_sc, m_sc, l_sc, acc_sc, *, n_blocks):
    i = pl.program_id(2)
    tq = ATT_BLK

    @pl.when(i == 0)
    def _():
        _store_key_norms(k_ref, knm_sc, n_blocks)
        for g in range(ATT_GROUPS):
            for j in range(n_blocks):
                kj = k_ref[0, j * MOBA_BLOCK:(j + 1) * MOBA_BLOCK, g * LANES:(g + 1) * LANES].astype(F32)
                kmean_sc[g, j:j + 1, :] = jnp.mean(kj, axis=0, keepdims=True)

    q_rows = []
    for g in range(ATT_GROUPS):
        qs = _split_heads(q_ref[0, :, g * LANES:(g + 1) * LANES])
        q_rows.append(qs)
        gate_t = lax.dot_general(kmean_sc[g].astype(BF16), qs, _NT, preferred_element_type=F32)
        blk = lax.broadcasted_iota(jnp.int32, gate_t.shape, 0)
        cnt = jnp.zeros(gate_t.shape, F32)
        for jp in range(n_blocks):
            row = gate_t[jp:jp + 1, :]
            beats = (row > gate_t) | ((row == gate_t) & (jp < blk))
            cnt = cnt + jnp.where(beats, jnp.where(jp < i, 1.0, 0.0), 0.0)
        selected = (cnt < MOBA_TOPK) & (blk < i)
        bias_t = jnp.where(selected | (blk == i), 0.0, NEG)
        for n in range(n_blocks):
            bias_sc[g, n] = bias_t[n:n + 1, :]

    o_t = _attend(q_rows, k_ref, vt_ref, bias_sc, knm_sc, i, m_sc, l_sc, acc_sc)
    for g in range(ATT_GROUPS):
        o_sel = jnp.concatenate([o_t[g][:HEAD_DIM, :tq], o_t[g][HEAD_DIM:, tq:]], axis=0)
        o_ref[0, :, g * LANES:(g + 1) * LANES] = o_sel.T.astype(o_ref.dtype)


def _moba_attention(qk, vt, d_model):
    b, s, _ = qk.shape
    gw = ATT_GROUPS * LANES
    n_steps = d_model // gw
    n_blocks = s // MOBA_BLOCK
    return pl.pallas_call(
        functools.partial(_moba_kernel, n_blocks=n_blocks),
        grid=(b, n_steps, n_blocks),
        in_specs=[
            pl.BlockSpec((1, ATT_BLK, gw), lambda bi, g, i: (bi, i, g)),
            pl.BlockSpec((1, s, gw), lambda bi, g, i: (bi, 0, n_steps + g)),
            pl.BlockSpec((1, n_blocks, gw, ATT_BLK), lambda bi, g, i: (bi, 0, g, 0)),
        ],
        out_specs=pl.BlockSpec((1, ATT_BLK, gw), lambda bi, g, i: (bi, i, g)),
        out_shape=jax.ShapeDtypeStruct((b, s, d_model), BF16),
        scratch_shapes=[pltpu.VMEM((ATT_GROUPS, n_blocks, LANES), F32),
                        pltpu.VMEM((ATT_GROUPS, n_blocks, 1, 2 * ATT_BLK), F32),
                        ] + _attn_scratch(),
        compiler_params=_cparams(("parallel", "parallel", "arbitrary")),
        name="moba_attn",
    )(qk, qk, vt)


def _diff_kernel(q_ref, k_ref, vt_ref, lam_ref, g_ref, o_ref, knm_sc, m_sc, l_sc, acc_sc, *, lambda_init, n_blocks):
    i = pl.program_id(2)
    tq = ATT_BLK

    @pl.when(i == 0)
    def _():
        _store_key_norms(k_ref, knm_sc, n_blocks)

    q_rows = [_split_heads(q_ref[0, :, g * LANES:(g + 1) * LANES])
              for g in range(ATT_GROUPS)]
    o_t = _attend(q_rows, k_ref, vt_ref, None, knm_sc, i, m_sc, l_sc, acc_sc)

    lp = lam_ref[...].astype(F32)
    lam = (jnp.exp(jnp.sum(lp[0:1] * lp[1:2], axis=1, keepdims=True))
           - jnp.exp(jnp.sum(lp[2:3] * lp[3:4], axis=1, keepdims=True)) + lambda_init)
    for g in range(ATT_GROUPS):
        o = (o_t[g][:, :tq] - lam * o_t[g][:, tq:]).T
        o = o * lax.rsqrt(jnp.mean(o * o, axis=1, keepdims=True) + SUBLN_EPS)
        o_ref[0, :, g * LANES:(g + 1) * LANES] = (o * g_ref[...] * (1.0 - lambda_init)).astype(o_ref.dtype)


def _diff_attention(q, k, vt, lam_params, subln_g, lambda_init, d_model):
    b, s, _ = q.shape
    gw = ATT_GROUPS * LANES
    n_steps = d_model // gw
    n_blocks = s // ATT_BLK
    return pl.pallas_call(
        functools.partial(_diff_kernel, lambda_init=lambda_init, n_blocks=n_blocks),
        grid=(b, n_steps, n_blocks),
        in_specs=[
            pl.BlockSpec((1, ATT_BLK, gw), lambda bi, h, i: (bi, i, h)),
            pl.BlockSpec((1, s, gw), lambda bi, h, i: (bi, 0, h)),
            pl.BlockSpec((1, n_blocks, gw, ATT_BLK), lambda bi, h, i: (bi, 0, h, 0)),
            pl.BlockSpec(lam_params.shape, lambda bi, h, i: (0, 0)),
            pl.BlockSpec((1, LANES), lambda bi, h, i: (0, 0)),
        ],
        out_specs=pl.BlockSpec((1, ATT_BLK, gw), lambda bi, h, i: (bi, i, h)),
        out_shape=jax.ShapeDtypeStruct((b, s, d_model), BF16),
        scratch_shapes=_attn_scratch(),
        compiler_params=_cparams(("parallel", "parallel", "arbitrary")),
        name="diff_attn",
    )(q, k, vt, lam_params, subln_g)


def _layer_norm(z, g, b):
    mu = jnp.mean(z, axis=1, keepdims=True)
    zc = z - mu
    var = jnp.mean(zc * zc, axis=1, keepdims=True)
    return zc * lax.rsqrt(var + LN_EPS) * g + b


def _oproj_ln_kernel(x_ref, a_ref, w_ref, g_ref, b_ref, o_ref, *, alpha):
    rows = x_ref.shape[0] // OPROJ_ROW_GROUPS
    for r0 in range(0, x_ref.shape[0], rows):
        y = jnp.dot(a_ref[r0:r0 + rows, :], w_ref[...], preferred_element_type=F32)
        o_ref[r0:r0 + rows, :] = _layer_norm(alpha * x_ref[r0:r0 + rows, :] + y, g_ref[...], b_ref[...])


def _oproj_ln(x2d, a2d, w_all, layer, g, b, alpha):
    t, d = x2d.shape
    tm = PROJ_TM
    return pl.pallas_call(
        functools.partial(_oproj_ln_kernel, alpha=alpha),
        grid=(t // tm,),
        in_specs=[
            pl.BlockSpec((tm, d), lambda i: (i, 0)),
            pl.BlockSpec((tm, d), lambda i: (i, 0)),
            pl.BlockSpec((None, d, d), lambda i: (layer, 0, 0)),
            pl.BlockSpec((1, d), lambda i: (0, 0)),
            pl.BlockSpec((1, d), lambda i: (0, 0)),
        ],
        out_specs=pl.BlockSpec((tm, d), lambda i: (i, 0)),
        out_shape=jax.ShapeDtypeStruct((t, d), F32),
        compiler_params=_cparams(("parallel",)),
        name="oproj_ln",
    )(x2d, a2d, w_all, g, b)


def _gelu_tanh(x):
    return 0.5 * x * (1.0 + jnp.tanh(math.sqrt(2.0 / math.pi) * (x + 0.044715 * (x * x * x))))


def _ffn_kernel(x_ref, halo_ref, wi_ref, cw_ref, cb_ref, wo_ref, g_ref, b_ref, o_ref, up_sc,
                *, alpha, tiles_per_seq, tf):
    i = pl.program_id(0)
    tm = x_ref.shape[0]
    d_ff = wo_ref.shape[0]
    x = x_ref[...]
    xb = x.astype(BF16)
    seq_start = (i % tiles_per_seq) == 0
    halo = jnp.where(seq_start, 0.0, halo_ref[...]).astype(BF16)
    x_ext = jnp.concatenate([halo, xb], axis=0)
    f = None
    for c0 in range(0, d_ff, tf):
        up_sc[...] = jnp.dot(x_ext, wi_ref[:, c0:c0 + tf], preferred_element_type=F32)
        gate = jnp.dot(xb, wi_ref[:, d_ff + c0:d_ff + c0 + tf], preferred_element_type=F32)
        cw = cw_ref[:, c0:c0 + tf]
        conv = (up_sc[FFN_HALO - 2:FFN_HALO - 2 + tm, :] * cw[0:1]
                + up_sc[FFN_HALO - 1:FFN_HALO - 1 + tm, :] * cw[1:2]
                + up_sc[FFN_HALO:FFN_HALO + tm, :] * cw[2:3]
                + cb_ref[:, c0:c0 + tf])
        h = (_gelu_tanh(conv) * gate).astype(BF16)
        part = jnp.dot(h, wo_ref[c0:c0 + tf, :], preferred_element_type=F32)
        f = part if f is None else f + part
    o_ref[...] = _layer_norm(alpha * x + f, g_ref[...], b_ref[...])


def _ffn_tile(d_ff):
    n = d_ff // LANES
    for parts in range(2, n + 1):
        if n % parts == 0:
            return d_ff // parts
    return d_ff


def _conv_ffn_ln(x2d, w_in_all, conv_w, conv_b, w_out_all, layer, g, b, alpha, seq):
    t, d = x2d.shape
    d_ff = w_out_all.shape[1]
    tm = FFN_TM
    tf = _ffn_tile(d_ff)
    halo_blocks = tm // FFN_HALO
    resident = pl.Buffered(1)
    return pl.pallas_call(
        functools.partial(_ffn_kernel, alpha=alpha, tiles_per_seq=seq // tm, tf=tf),
        grid=(t // tm,),
        in_specs=[
            pl.BlockSpec((tm, d), lambda i: (i, 0)),
            pl.BlockSpec((FFN_HALO, d), lambda i: (jnp.maximum(i * halo_blocks - 1, 0), 0)),
            pl.BlockSpec((None, d, 2 * d_ff), lambda i: (layer, 0, 0), pipeline_mode=resident),
            pl.BlockSpec((CONV_WIDTH, d_ff), lambda i: (0, 0)),
            pl.BlockSpec((1, d_ff), lambda i: (0, 0)),
            pl.BlockSpec((None, d_ff, d), lambda i: (layer, 0, 0), pipeline_mode=resident),
            pl.BlockSpec((1, d), lambda i: (0, 0)),
            pl.BlockSpec((1, d), lambda i: (0, 0)),
        ],
        out_specs=pl.BlockSpec((tm, d), lambda i: (i, 0)),
        out_shape=jax.ShapeDtypeStruct((t, d), F32),
        scratch_shapes=[pltpu.VMEM((FFN_HALO + tm, tf), F32)],
        compiler_params=_cparams(("parallel",)),
        name="conv_ffn_ln",
    )(x2d, x2d, w_in_all, conv_w, conv_b, w_out_all, g, b)


def _rotary_tables(positions):
    inv_freq = 1.0 / (ROPE_THETA ** (jnp.arange(ROPE_HALF, dtype=F32) * 2.0 / ROPE_ROT))
    ang = positions.astype(F32).reshape(-1, 1) * inv_freq
    cos, sin = jnp.cos(ang), jnp.sin(ang)
    t = ang.shape[0]
    ones = jnp.ones((t, HEAD_DIM - ROPE_ROT), F32)
    zeros = jnp.zeros((t, HEAD_DIM - ROPE_ROT), F32)
    c_head = jnp.concatenate([cos, cos, ones], axis=1)
    s_head = jnp.concatenate([-sin, sin, zeros], axis=1)
    reps = LANES // HEAD_DIM
    return jnp.tile(c_head, (1, reps)), jnp.tile(s_head, (1, reps))


def kernel(x, positions, a_w_qkv, a_w_o, w_kv_shared, b_w_q, b_w_o, b_lambda, b_subln_g, ln_g, ln_b,
           ffn_w_in, ffn_conv_w, ffn_conv_b, ffn_w_out):
    bsz, seq, d = x.shape
    depth = ffn_w_in.shape[0]
    n_a = a_w_qkv.shape[0]
    t = bsz * seq
    alpha = (2 * depth) ** 0.25
    assert d % (ATT_GROUPS * LANES) == 0
    assert seq % max(PROJ_TM, FFN_TM, ATT_CHUNK * ATT_BLK, PROJ_T_BLOCKS * ATT_BLK) == 0

    cos_tab, sin_tab = _rotary_tables(positions)
    a_w_qkv_b, a_w_o_b = a_w_qkv.astype(BF16), a_w_o.astype(BF16)
    a_wv_t = jnp.swapaxes(a_w_qkv[:, :, 2 * d:], 1, 2).astype(BF16)
    w_kv_b = w_kv_shared.astype(BF16)[None]
    w_v_sh_t = w_kv_shared[:, d:].T.astype(BF16)[None]
    b_w_q_b, b_w_o_b = b_w_q.astype(BF16), b_w_o.astype(BF16)
    ffn_w_in_b, ffn_w_out_b = ffn_w_in.astype(BF16), ffn_w_out.astype(BF16)

    xs = x.reshape(t, d)
    k_sh = vt_sh = None
    for layer in range(depth):
        g0, b0 = ln_g[layer, 0].reshape(1, d), ln_b[layer, 0].reshape(1, d)
        g1, b1 = ln_g[layer, 1].reshape(1, d), ln_b[layer, 1].reshape(1, d)
        if layer < n_a:
            qk = _proj_rot(xs, a_w_qkv_b, layer, 2 * d, cos_tab, sin_tab, n_scaled=d)
            vt = _proj_t(xs.reshape(bsz, seq, d), a_wv_t, layer)
            o = _moba_attention(qk.reshape(bsz, seq, 2 * d), vt, d)
            xs = _oproj_ln(xs, o.reshape(t, d), a_w_o_b, layer, g0, b0, alpha)
        else:
            jb = layer - n_a
            if jb == 0:
                k_sh = _proj_rot(xs, w_kv_b, 0, d, cos_tab, sin_tab, n_scaled=0).reshape(bsz, seq, d)
                vt_sh = _proj_t(xs.reshape(bsz, seq, d), w_v_sh_t, 0)
            lambda_init = 0.8 - 0.6 * math.exp(-0.3 * layer)
            q = _proj_rot(xs, b_w_q_b, jb, d, cos_tab, sin_tab, n_scaled=d)
            o = _diff_attention(q.reshape(bsz, seq, d), k_sh, vt_sh, b_lambda[jb],
                                b_subln_g[jb].reshape(1, LANES), lambda_init, d)
            xs = _oproj_ln(xs, o.reshape(t, d), b_w_o_b, jb, g0, b0, alpha)
        xs = _conv_ffn_ln(xs, ffn_w_in_b, ffn_conv_w[layer], ffn_conv_b[layer].reshape(1, -1),
                          ffn_w_out_b, layer, g1, b1, alpha, seq)
    return xs.reshape(bsz, seq, d)
```

```python
import functools
import math

import jax
import jax.numpy as jnp
from jax import lax
from jax.experimental import pallas as pl
from jax.experimental.pallas import tpu as pltpu

F32 = jnp.float32
BF16 = jnp.bfloat16

HEAD_DIM = 64
LANES = 128
SUBLANES = 8
MOBA_BLOCK = 256
MOBA_TOPK = 3
ROPE_THETA = 500000.0
ROPE_ROT = HEAD_DIM // 4
ROPE_HALF = ROPE_ROT // 2
LN_EPS = 1e-5
SUBLN_EPS = 1e-5
CONV_WIDTH = 3
NEG = -1e30

VMEM_LIMIT = 56 * 1024 * 1024
PROJ_TM = 1024
PROJ_PIECE = 512
PROJ_T_BLOCKS = 4
FFN_TM = 512
FFN_HALO = 16
OPROJ_ROW_GROUPS = 4
ATT_BLK = MOBA_BLOCK
ATT_CHUNK = 4
ATT_GROUPS = 8
Q_SCALE = HEAD_DIM ** -0.5 * math.log2(math.e)
BOUND_SLACK = 1.05
FAST_BOUND = 32.0

_NT = (((1,), (1,)), ((), ()))


def _cparams(sem):
    return pltpu.CompilerParams(dimension_semantics=sem, vmem_limit_bytes=VMEM_LIMIT)


def _proj_rot_kernel(x_ref, w_ref, c_ref, s_ref, o_ref, *, n_scaled):
    xb = x_ref[...].astype(BF16)
    tm, n = o_ref.shape
    c = c_ref[...]
    s = s_ref[...]
    lane = lax.broadcasted_iota(jnp.int32, (tm, LANES), 1)
    first_half = (lane & (HEAD_DIM - 1)) < ROPE_HALF
    for c0 in range(0, n, PROJ_PIECE):
        y = jnp.dot(xb, w_ref[:, c0:c0 + PROJ_PIECE], preferred_element_type=F32)
        if c0 < n_scaled:
            y = y * Q_SCALE
        cols = []
        for g in range(PROJ_PIECE // LANES):
            yg = y[:, g * LANES:(g + 1) * LANES]
            partner = jnp.where(first_half, pltpu.roll(yg, LANES - ROPE_HALF, 1), pltpu.roll(yg, ROPE_HALF, 1))
            cols.append(yg * c + partner * s)
        o_ref[:, c0:c0 + PROJ_PIECE] = jnp.concatenate(cols, axis=1).astype(o_ref.dtype)


def _proj_rot(x2d, w_all, layer, n, cos_tab, sin_tab, n_scaled):
    t, k = x2d.shape
    tm = PROJ_TM
    assert n % PROJ_PIECE == 0 and n_scaled % PROJ_PIECE == 0
    return pl.pallas_call(
        functools.partial(_proj_rot_kernel, n_scaled=n_scaled),
        grid=(t // tm,),
        in_specs=[
            pl.BlockSpec((tm, k), lambda i: (i, 0)),
            pl.BlockSpec((None, k, n), lambda i: (layer, 0, 0), pipeline_mode=pl.Buffered(1)),
            pl.BlockSpec((tm, LANES), lambda i: (i, 0)),
            pl.BlockSpec((tm, LANES), lambda i: (i, 0)),
        ],
        out_specs=pl.BlockSpec((tm, n), lambda i: (i, 0)),
        out_shape=jax.ShapeDtypeStruct((t, n), BF16),
        compiler_params=_cparams(("parallel",)),
        name="proj_rot",
    )(x2d, w_all, cos_tab, sin_tab)


def _proj_t_kernel(x_ref, wt_ref, o_ref):
    xb = x_ref[0].astype(BF16)
    y = lax.dot_general(wt_ref[...], xb, _NT, preferred_element_type=F32)
    for j in range(PROJ_T_BLOCKS):
        o_ref[0, j] = y[:, j * ATT_BLK:(j + 1) * ATT_BLK].astype(o_ref.dtype)


def _proj_t(x3d, wt_all, layer):
    b, s, d = x3d.shape
    n = wt_all.shape[1]
    rows = PROJ_T_BLOCKS * ATT_BLK
    return pl.pallas_call(
        _proj_t_kernel,
        grid=(b, s // rows),
        in_specs=[
            pl.BlockSpec((1, rows, d), lambda bi, j: (bi, j, 0)),
            pl.BlockSpec((None, n, d), lambda bi, j: (layer, 0, 0)),
        ],
        out_specs=pl.BlockSpec((1, PROJ_T_BLOCKS, n, ATT_BLK), lambda bi, j: (bi, j, 0, 0)),
        out_shape=jax.ShapeDtypeStruct((b, s // ATT_BLK, n, ATT_BLK), BF16),
        compiler_params=_cparams(("parallel", "parallel")),
        name="proj_t",
    )(x3d, wt_all)


def _split_heads(q2):
    lane = lax.broadcasted_iota(jnp.int32, q2.shape, 1)
    zero = jnp.zeros_like(q2)
    q_a = jnp.where(lane < HEAD_DIM, q2, zero)
    q_b = jnp.where(lane >= HEAD_DIM, q2, zero)
    return jnp.concatenate([q_a, q_b], axis=0)


def _head_selector():
    r = lax.broadcasted_iota(jnp.int32, (SUBLANES, LANES), 0)
    lane = lax.broadcasted_iota(jnp.int32, (SUBLANES, LANES), 1)
    return jnp.where(((r == 0) & (lane < HEAD_DIM)) | ((r == 1) & (lane >= HEAD_DIM)), 1.0, 0.0).astype(BF16)


def _sq_norms_t(sel, x):
    xf = x.astype(F32)
    return lax.dot_general(sel, (xf * xf).astype(BF16), _NT, preferred_element_type=F32)


def _store_key_norms(k_ref, knm_sc, n_blocks):
    sel = _head_selector()
    for g in range(ATT_GROUPS):
        kn2 = jnp.zeros((SUBLANES, 1), F32)
        for j in range(n_blocks):
            kj = k_ref[0, j * ATT_BLK:(j + 1) * ATT_BLK, g * LANES:(g + 1) * LANES]
            kn2 = jnp.maximum(kn2, jnp.max(_sq_norms_t(sel, kj), axis=1, keepdims=True))
        knm_sc[g] = jnp.broadcast_to(kn2, (SUBLANES, LANES))


def _attend(q_rows, k_ref, vt_ref, bias_sc, knm_sc, i, m_sc, l_sc, acc_sc):
    tk = ATT_CHUNK * ATT_BLK
    groups = range(len(q_rows))
    n_cols = 2 * ATT_BLK

    def scores(g, start, size):
        k_c = k_ref[0, pl.ds(start, size), g * LANES:(g + 1) * LANES]
        return lax.dot_general(k_c, q_rows[g], _NT, preferred_element_type=F32)

    def add_rows(g, s_t, blk0, n, neg_shift):
        if bias_sc is None:
            return s_t if neg_shift is None else s_t - neg_shift
        s3 = s_t.reshape(n, ATT_BLK, n_cols) + bias_sc[g, pl.ds(blk0, n)]
        return s3.reshape(n * ATT_BLK, n_cols)

    def vt_cols(g, blk0, n):
        return jnp.concatenate([vt_ref[0, blk0 + b, g * LANES:(g + 1) * LANES, :] for b in range(n)], axis=1)

    def causal(s_own):
        key_pos = lax.broadcasted_iota(jnp.int32, s_own.shape, 0)
        q_pos = lax.broadcasted_iota(jnp.int32, s_own.shape, 1) & (ATT_BLK - 1)
        return jnp.where(key_pos <= q_pos, s_own, NEG)

    def sublane_sums(p):
        return jnp.sum(p.reshape(-1, SUBLANES, n_cols), axis=0)

    ones_row = (lax.broadcasted_iota(jnp.int32, (SUBLANES, LANES), 0) == 0).astype(BF16)
    col = lax.broadcasted_iota(jnp.int32, (1, n_cols), 1)
    bound = []
    for g in groups:
        qn2 = _sq_norms_t(ones_row, q_rows[g])[0:1]
        kn2 = jnp.where(col < ATT_BLK, knm_sc[g, 0:1, 0:1], knm_sc[g, 1:2, 0:1])
        bound.append(jnp.sqrt(qn2 * kn2) * BOUND_SLACK)
    worst = functools.reduce(jnp.maximum, [jnp.max(b) for b in bound])
    fast = worst <= FAST_BOUND

    def fast_tail(n_tail):
        blk0 = i - n_tail
        start = pl.multiple_of(blk0 * ATT_BLK, ATT_BLK)
        for g in groups:
            s_t = scores(g, start, (n_tail + 1) * ATT_BLK)
            s_own = causal(s_t[n_tail * ATT_BLK:])
            s_t = jnp.concatenate([s_t[:n_tail * ATT_BLK], s_own], axis=0) if n_tail else s_own
            p = jnp.exp2(add_rows(g, s_t, blk0, n_tail + 1, bound[g]))
            l_sc[g] = sublane_sums(p)
            acc_sc[g] = jnp.dot(vt_cols(g, blk0, n_tail + 1), p.astype(BF16), preferred_element_type=F32)

    def fast_body(c, carry):
        start = pl.multiple_of(c * tk, tk)
        for g in groups:
            p = jnp.exp2(add_rows(g, scores(g, start, tk), c * ATT_CHUNK, ATT_CHUNK, bound[g]))
            l_sc[g] += sublane_sums(p)
            acc_sc[g] += jnp.dot(vt_cols(g, c * ATT_CHUNK, ATT_CHUNK), p.astype(BF16),
                                 preferred_element_type=F32)
        return carry

    @pl.when(fast)
    def _():
        if bias_sc is not None:
            for g in groups:
                bias_sc[g] = bias_sc[g] - bound[g]
        for n_tail in range(ATT_CHUNK):
            pl.when((i & (ATT_CHUNK - 1)) == n_tail)(functools.partial(fast_tail, n_tail))
        lax.fori_loop(0, i // ATT_CHUNK, fast_body, 0)
        for g in groups:
            acc_sc[g] = acc_sc[g] / jnp.sum(l_sc[g], axis=0, keepdims=True)

    @pl.when(jnp.logical_not(fast))
    def _():
        own = pl.multiple_of(i * ATT_BLK, ATT_BLK)
        for g in groups:
            s_t = causal(scores(g, own, ATT_BLK))
            m = jnp.max(s_t, axis=0, keepdims=True)
            p = jnp.exp2(s_t - m)
            m_sc[g] = m
            l_sc[g, 0:1, :] = jnp.sum(p, axis=0, keepdims=True)
            acc_sc[g] = jnp.dot(vt_cols(g, i, 1), p.astype(BF16), preferred_element_type=F32)

        def exact_body(j, carry):
            start = pl.multiple_of(j * ATT_BLK, ATT_BLK)
            for g in groups:
                s_j = add_rows(g, scores(g, start, ATT_BLK), j, 1, None)
                m_old = m_sc[g]
                m_new = jnp.maximum(m_old, jnp.max(s_j, axis=0, keepdims=True))
                alpha = jnp.exp2(m_old - m_new)
                p = jnp.exp2(s_j - m_new)
                l_sc[g, 0:1, :] = alpha * l_sc[g, 0:1, :] + jnp.sum(p, axis=0, keepdims=True)
                acc_sc[g] = alpha * acc_sc[g] + jnp.dot(vt_cols(g, j, 1), p.astype(BF16),
                                                        preferred_element_type=F32)
                m_sc[g] = m_new
            return carry

        lax.fori_loop(0, i, exact_body, 0)
        for g in groups:
            acc_sc[g] = acc_sc[g] / l_sc[g, 0:1, :]

    return [acc_sc[g] for g in groups]


def _attn_scratch():
    return [
        pltpu.VMEM((ATT_GROUPS, SUBLANES, LANES), F32),
        pltpu.VMEM((ATT_GROUPS, 1, 2 * ATT_BLK), F32),
        pltpu.VMEM((ATT_GROUPS, SUBLANES, 2 * ATT_BLK), F32),
        pltpu.VMEM((ATT_GROUPS, LANES, 2 * ATT_BLK), F32),
    ]


def _moba_kernel(q_ref, k_ref, vt_ref, o_ref, kmean_sc, bias_sc, knm_sc, m_sc, l_sc, acc_sc, *, n_blocks):
    i = pl.program_id(2)
    tq = ATT_BLK

    @pl.when(i == 0)
    def _():
        _store_key_norms(k_ref, knm_sc, n_blocks)
        for g in range(ATT_GROUPS):
            for j in range(n_blocks):
                kj = k_ref[0, j * MOBA_BLOCK:(j + 1) * MOBA_BLOCK, g * LANES:(g + 1) * LANES].astype(F32)
                kmean_sc[g, j:j + 1, :] = jnp.mean(kj, axis=0, keepdims=True)

    q_rows = []
    for g in range(ATT_GROUPS):
        qs = _split_heads(q_ref[0, :, g * LANES:(g + 1) * LANES])
        q_rows.append(qs)
        gate_t = lax.dot_general(kmean_sc[g].astype(BF16), qs, _NT, preferred_element_type=F32)
        blk = lax.broadcasted_iota(jnp.int32, gate_t.shape, 0)
        cnt = jnp.zeros(gate_t.shape, F32)
        for jp in range(n_blocks):
            row = gate_t[jp:jp + 1, :]
            beats = (row > gate_t) | ((row == gate_t) & (jp < blk))
            cnt = cnt + jnp.where(beats, jnp.where(jp < i, 1.0, 0.0), 0.0)
        selected = (cnt < MOBA_TOPK) & (blk < i)
        bias_t = jnp.where(selected | (blk == i), 0.0, NEG)
        for n in range(n_blocks):
            bias_sc[g, n] = bias_t[n:n + 1, :]

    o_t = _attend(q_rows, k_ref, vt_ref, bias_sc, knm_sc, i, m_sc, l_sc, acc_sc)
    for g in range(ATT_GROUPS):
        o_sel = jnp.concatenate([o_t[g][:HEAD_DIM, :tq], o_t[g][HEAD_DIM:, tq:]], axis=0)
        o_ref[0, :, g * LANES:(g + 1) * LANES] = o_sel.T.astype(o_ref.dtype)


def _moba_attention(qk, vt, d_model):
    b, s, _ = qk.shape
    gw = ATT_GROUPS * LANES
    n_steps = d_model // gw
    n_blocks = s // MOBA_BLOCK
    return pl.pallas_call(
        functools.partial(_moba_kernel, n_blocks=n_blocks),
        grid=(b, n_steps, n_blocks),
        in_specs=[
            pl.BlockSpec((1, ATT_BLK, gw), lambda bi, g, i: (bi, i, g)),
            pl.BlockSpec((1, s, gw), lambda bi, g, i: (bi, 0, n_steps + g)),
            pl.BlockSpec((1, n_blocks, gw, ATT_BLK), lambda bi, g, i: (bi, 0, g, 0)),
        ],
        out_specs=pl.BlockSpec((1, ATT_BLK, gw), lambda bi, g, i: (bi, i, g)),
        out_shape=jax.ShapeDtypeStruct((b, s, d_model), BF16),
        scratch_shapes=[pltpu.VMEM((ATT_GROUPS, n_blocks, LANES), F32),
                        pltpu.VMEM((ATT_GROUPS, n_blocks, 1, 2 * ATT_BLK), F32),
                        ] + _attn_scratch(),
        compiler_params=_cparams(("parallel", "parallel", "arbitrary")),
        name="moba_attn",
    )(qk, qk, vt)


def _diff_kernel(q_ref, k_ref, vt_ref, lam_ref, g_ref, o_ref, knm_sc, m_sc, l_sc, acc_sc, *, lambda_init, n_blocks):
    i = pl.program_id(2)
    tq = ATT_BLK

    @pl.when(i == 0)
    def _():
        _store_key_norms(k_ref, knm_sc, n_blocks)

    q_rows = [_split_heads(q_ref[0, :, g * LANES:(g + 1) * LANES])
              for g in range(ATT_GROUPS)]
    o_t = _attend(q_rows, k_ref, vt_ref, None, knm_sc, i, m_sc, l_sc, acc_sc)

    lp = lam_ref[...].astype(F32)
    lam = (jnp.exp(jnp.sum(lp[0:1] * lp[1:2], axis=1, keepdims=True))
           - jnp.exp(jnp.sum(lp[2:3] * lp[3:4], axis=1, keepdims=True)) + lambda_init)
    for g in range(ATT_GROUPS):
        o = (o_t[g][:, :tq] - lam * o_t[g][:, tq:]).T
        o = o * lax.rsqrt(jnp.mean(o * o, axis=1, keepdims=True) + SUBLN_EPS)
        o_ref[0, :, g * LANES:(g + 1) * LANES] = (o * g_ref[...] * (1.0 - lambda_init)).astype(o_ref.dtype)


def _diff_attention(q, k, vt, lam_params, subln_g, lambda_init, d_model):
    b, s, _ = q.shape
    gw = ATT_GROUPS * LANES
    n_steps = d_model // gw
    n_blocks = s // ATT_BLK
    return pl.pallas_call(
        functools.partial(_diff_kernel, lambda_init=lambda_init, n_blocks=n_blocks),
        grid=(b, n_steps, n_blocks),
        in_specs=[
            pl.BlockSpec((1, ATT_BLK, gw), lambda bi, h, i: (bi, i, h)),
            pl.BlockSpec((1, s, gw), lambda bi, h, i: (bi, 0, h)),
            pl.BlockSpec((1, n_blocks, gw, ATT_BLK), lambda bi, h, i: (bi, 0, h, 0)),
            pl.BlockSpec(lam_params.shape, lambda bi, h, i: (0, 0)),
            pl.BlockSpec((1, LANES), lambda bi, h, i: (0, 0)),
        ],
        out_specs=pl.BlockSpec((1, ATT_BLK, gw), lambda bi, h, i: (bi, i, h)),
        out_shape=jax.ShapeDtypeStruct((b, s, d_model), BF16),
        scratch_shapes=_attn_scratch(),
        compiler_params=_cparams(("parallel", "parallel", "arbitrary")),
        name="diff_attn",
    )(q, k, vt, lam_params, subln_g)


def _layer_norm(z, g, b):
    mu = jnp.mean(z, axis=1, keepdims=True)
    zc = z - mu
    var = jnp.mean(zc * zc, axis=1, keepdims=True)
    return zc * lax.rsqrt(var + LN_EPS) * g + b


def _oproj_ln_kernel(x_ref, a_ref, w_ref, g_ref, b_ref, o_ref, *, alpha):
    rows = x_ref.shape[0] // OPROJ_ROW_GROUPS
    for r0 in range(0, x_ref.shape[0], rows):
        y = jnp.dot(a_ref[r0:r0 + rows, :], w_ref[...], preferred_element_type=F32)
        o_ref[r0:r0 + rows, :] = _layer_norm(alpha * x_ref[r0:r0 + rows, :] + y, g_ref[...], b_ref[...])


def _oproj_ln(x2d, a2d, w_all, layer, g, b, alpha):
    t, d = x2d.shape
    tm = PROJ_TM
    return pl.pallas_call(
        functools.partial(_oproj_ln_kernel, alpha=alpha),
        grid=(t // tm,),
        in_specs=[
            pl.BlockSpec((tm, d), lambda i: (i, 0)),
            pl.BlockSpec((tm, d), lambda i: (i, 0)),
            pl.BlockSpec((None, d, d), lambda i: (layer, 0, 0)),
            pl.BlockSpec((1, d), lambda i: (0, 0)),
            pl.BlockSpec((1, d), lambda i: (0, 0)),
        ],
        out_specs=pl.BlockSpec((tm, d), lambda i: (i, 0)),
        out_shape=jax.ShapeDtypeStruct((t, d), F32),
        compiler_params=_cparams(("parallel",)),
        name="oproj_ln",
    )(x2d, a2d, w_all, g, b)


def _gelu_tanh(x):
    return 0.5 * x * (1.0 + jnp.tanh(math.sqrt(2.0 / math.pi) * (x + 0.044715 * (x * x * x))))


def _ffn_kernel(x_ref, halo_ref, wi_ref, cw_ref, cb_ref, wo_ref, g_ref, b_ref, o_ref, up_sc,
                *, alpha, tiles_per_seq, tf):
    i = pl.program_id(0)
    tm = x_ref.shape[0]
    d_ff = wo_ref.shape[0]
    x = x_ref[...]
    xb = x.astype(BF16)
    seq_start = (i % tiles_per_seq) == 0
    halo = jnp.where(seq_start, 0.0, halo_ref[...]).astype(BF16)
    x_ext = jnp.concatenate([halo, xb], axis=0)
    f = None
    for c0 in range(0, d_ff, tf):
        up_sc[...] = jnp.dot(x_ext, wi_ref[:, c0:c0 + tf], preferred_element_type=F32)
        gate = jnp.dot(xb, wi_ref[:, d_ff + c0:d_ff + c0 + tf], preferred_element_type=F32)
        cw = cw_ref[:, c0:c0 + tf]
        conv = (up_sc[FFN_HALO - 2:FFN_HALO - 2 + tm, :] * cw[0:1]
                + up_sc[FFN_HALO - 1:FFN_HALO - 1 + tm, :] * cw[1:2]
                + up_sc[FFN_HALO:FFN_HALO + tm, :] * cw[2:3]
                + cb_ref[:, c0:c0 + tf])
        h = (_gelu_tanh(conv) * gate).astype(BF16)
        part = jnp.dot(h, wo_ref[c0:c0 + tf, :], preferred_element_type=F32)
        f = part if f is None else f + part
    o_ref[...] = _layer_norm(alpha * x + f, g_ref[...], b_ref[...])


def _ffn_tile(d_ff):
    n = d_ff // LANES
    for parts in range(2, n + 1):
        if n % parts == 0:
            return d_ff // parts
    return d_ff


def _conv_ffn_ln(x2d, w_in_all, conv_w, conv_b, w_out_all, layer, g, b, alpha, seq):
    t, d = x2d.shape
    d_ff = w_out_all.shape[1]
    tm = FFN_TM
    tf = _ffn_tile(d_ff)
    halo_blocks = tm // FFN_HALO
    resident = pl.Buffered(1)
    return pl.pallas_call(
        functools.partial(_ffn_kernel, alpha=alpha, tiles_per_seq=seq // tm, tf=tf),
        grid=(t // tm,),
        in_specs=[
            pl.BlockSpec((tm, d), lambda i: (i, 0)),
            pl.BlockSpec((FFN_HALO, d), lambda i: (jnp.maximum(i * halo_blocks - 1, 0), 0)),
            pl.BlockSpec((None, d, 2 * d_ff), lambda i: (layer, 0, 0), pipeline_mode=resident),
            pl.BlockSpec((CONV_WIDTH, d_ff), lambda i: (0, 0)),
            pl.BlockSpec((1, d_ff), lambda i: (0, 0)),
            pl.BlockSpec((None, d_ff, d), lambda i: (layer, 0, 0), pipeline_mode=resident),
            pl.BlockSpec((1, d), lambda i: (0, 0)),
            pl.BlockSpec((1, d), lambda i: (0, 0)),
        ],
        out_specs=pl.BlockSpec((tm, d), lambda i: (i, 0)),
        out_shape=jax.ShapeDtypeStruct((t, d), F32),
        scratch_shapes=[pltpu.VMEM((FFN_HALO + tm, tf), F32)],
        compiler_params=_cparams(("parallel",)),
        name="conv_ffn_ln",
    )(x2d, x2d, w_in_all, conv_w, conv_b, w_out_all, g, b)


def _rotary_tables(positions):
    inv_freq = 1.0 / (ROPE_THETA ** (jnp.arange(ROPE_HALF, dtype=F32) * 2.0 / ROPE_ROT))
    ang = positions.astype(F32).reshape(-1, 1) * inv_freq
    cos, sin = jnp.cos(ang), jnp.sin(ang)
    t = ang.shape[0]
    ones = jnp.ones((t, HEAD_DIM - ROPE_ROT), F32)
    zeros = jnp.zeros((t, HEAD_DIM - ROPE_ROT), F32)
    c_head = jnp.concatenate([cos, cos, ones], axis=1)
    s_head = jnp.concatenate([-sin, sin, zeros], axis=1)
    reps = LANES // HEAD_DIM
    return jnp.tile(c_head, (1, reps)), jnp.tile(s_head, (1, reps))


def kernel(x, positions, a_w_qkv, a_w_o, w_kv_shared, b_w_q, b_w_o, b_lambda, b_subln_g, ln_g, ln_b,
           ffn_w_in, ffn_conv_w, ffn_conv_b, ffn_w_out):
    bsz, seq, d = x.shape
    depth = ffn_w_in.shape[0]
    n_a = a_w_qkv.shape[0]
    t = bsz * seq
    alpha = (2 * depth) ** 0.25
    assert d % (ATT_GROUPS * LANES) == 0
    assert seq % max(PROJ_TM, FFN_TM, ATT_CHUNK * ATT_BLK, PROJ_T_BLOCKS * ATT_BLK) == 0

    cos_tab, sin_tab = _rotary_tables(positions)
    a_w_qkv_b, a_w_o_b = a_w_qkv.astype(BF16), a_w_o.astype(BF16)
    a_wv_t = jnp.swapaxes(a_w_qkv[:, :, 2 * d:], 1, 2).astype(BF16)
    w_kv_b = w_kv_shared.astype(BF16)[None]
    w_v_sh_t = w_kv_shared[:, d:].T.astype(BF16)[None]
    b_w_q_b, b_w_o_b = b_w_q.astype(BF16), b_w_o.astype(BF16)
    ffn_w_in_b, ffn_w_out_b = ffn_w_in.astype(BF16), ffn_w_out.astype(BF16)

    xs = x.reshape(t, d)
    k_sh = vt_sh = None
    for layer in range(depth):
        g0, b0 = ln_g[layer, 0].reshape(1, d), ln_b[layer, 0].reshape(1, d)
        g1, b1 = ln_g[layer, 1].reshape(1, d), ln_b[layer, 1].reshape(1, d)
        if layer < n_a:
            qk = _proj_rot(xs, a_w_qkv_b, layer, 2 * d, cos_tab, sin_tab, n_scaled=d)
            vt = _proj_t(xs.reshape(bsz, seq, d), a_wv_t, layer)
            o = _moba_attention(qk.reshape(bsz, seq, 2 * d), vt, d)
            xs = _oproj_ln(xs, o.reshape(t, d), a_w_o_b, layer, g0, b0, alpha)
        else:
            jb = layer - n_a
            if jb == 0:
                k_sh = _proj_rot(xs, w_kv_b, 0, d, cos_tab, sin_tab, n_scaled=0).reshape(bsz, seq, d)
                vt_sh = _proj_t(xs.reshape(bsz, seq, d), w_v_sh_t, 0)
            lambda_init = 0.8 - 0.6 * math.exp(-0.3 * layer)
            q = _proj_rot(xs, b_w_q_b, jb, d, cos_tab, sin_tab, n_scaled=d)
            o = _diff_attention(q.reshape(bsz, seq, d), k_sh, vt_sh, b_lambda[jb],
                                b_subln_g[jb].reshape(1, LANES), lambda_init, d)
            xs = _oproj_ln(xs, o.reshape(t, d), b_w_o_b, jb, g0, b0, alpha)
        xs = _conv_ffn_ln(xs, ffn_w_in_b, ffn_conv_w[layer], ffn_conv_b[layer].reshape(1, -1),
                          ffn_w_out_b, layer, g1, b1, alpha, seq)
    return xs.reshape(bsz, seq, d)
```

```python
import functools
import math

import jax
import jax.numpy as jnp
from jax import lax
from jax.experimental import pallas as pl
from jax.experimental.pallas import tpu as pltpu

F32 = jnp.float32
BF16 = jnp.bfloat16

HEAD_DIM = 64
LANES = 128
SUBLANES = 8
MOBA_BLOCK = 256
MOBA_TOPK = 3
ROPE_THETA = 500000.0
ROPE_ROT = HEAD_DIM // 4
ROPE_HALF = ROPE_ROT // 2
LN_EPS = 1e-5
SUBLN_EPS = 1e-5
CONV_WIDTH = 3
NEG = -1e30

VMEM_LIMIT = 56 * 1024 * 1024
PROJ_TM = 1024
PROJ_PIECE = 512
PROJ_T_BLOCKS = 4
FFN_TM = 512
FFN_HALO = 16
OPROJ_ROW_GROUPS = 4
ATT_BLK = MOBA_BLOCK
ATT_CHUNK = 4
ATT_GROUPS = 8
Q_SCALE = HEAD_DIM ** -0.5 * math.log2(math.e)
BOUND_SLACK = 1.05
FAST_BOUND = 32.0

_NT = (((1,), (1,)), ((), ()))


def _cparams(sem):
    return pltpu.CompilerParams(dimension_semantics=sem, vmem_limit_bytes=VMEM_LIMIT)


def _proj_rot_kernel(x_ref, w_ref, c_ref, s_ref, o_ref, *, n_scaled):
    xb = x_ref[...].astype(BF16)
    tm, n = o_ref.shape
    c = c_ref[...]
    s = s_ref[...]
    lane = lax.broadcasted_iota(jnp.int32, (tm, LANES), 1)
    first_half = (lane & (HEAD_DIM - 1)) < ROPE_HALF
    for c0 in range(0, n, PROJ_PIECE):
        y = jnp.dot(xb, w_ref[:, c0:c0 + PROJ_PIECE], preferred_element_type=F32)
        if c0 < n_scaled:
            y = y * Q_SCALE
        cols = []
        for g in range(PROJ_PIECE // LANES):
            yg = y[:, g * LANES:(g + 1) * LANES]
            partner = jnp.where(first_half, pltpu.roll(yg, LANES - ROPE_HALF, 1), pltpu.roll(yg, ROPE_HALF, 1))
            cols.append(yg * c + partner * s)
        o_ref[:, c0:c0 + PROJ_PIECE] = jnp.concatenate(cols, axis=1).astype(o_ref.dtype)


def _proj_rot(x2d, w_all, layer, n, cos_tab, sin_tab, n_scaled):
    t, k = x2d.shape
    tm = PROJ_TM
    assert n % PROJ_PIECE == 0 and n_scaled % PROJ_PIECE == 0
    return pl.pallas_call(
        functools.partial(_proj_rot_kernel, n_scaled=n_scaled),
        grid=(t // tm,),
        in_specs=[
            pl.BlockSpec((tm, k), lambda i: (i, 0)),
            pl.BlockSpec((None, k, n), lambda i: (layer, 0, 0), pipeline_mode=pl.Buffered(1)),
            pl.BlockSpec((tm, LANES), lambda i: (i, 0)),
            pl.BlockSpec((tm, LANES), lambda i: (i, 0)),
        ],
        out_specs=pl.BlockSpec((tm, n), lambda i: (i, 0)),
        out_shape=jax.ShapeDtypeStruct((t, n), BF16),
        compiler_params=_cparams(("parallel",)),
        name="proj_rot",
    )(x2d, w_all, cos_tab, sin_tab)


def _proj_t_kernel(x_ref, wt_ref, o_ref):
    xb = x_ref[0].astype(BF16)
    y = lax.dot_general(wt_ref[...], xb, _NT, preferred_element_type=F32)
    for j in range(PROJ_T_BLOCKS):
        o_ref[0, j] = y[:, j * ATT_BLK:(j + 1) * ATT_BLK].astype(o_ref.dtype)


def _proj_t(x3d, wt_all, layer):
    b, s, d = x3d.shape
    n = wt_all.shape[1]
    rows = PROJ_T_BLOCKS * ATT_BLK
    return pl.pallas_call(
        _proj_t_kernel,
        grid=(b, s // rows),
        in_specs=[
            pl.BlockSpec((1, rows, d), lambda bi, j: (bi, j, 0)),
            pl.BlockSpec((None, n, d), lambda bi, j: (layer, 0, 0)),
        ],
        out_specs=pl.BlockSpec((1, PROJ_T_BLOCKS, n, ATT_BLK), lambda bi, j: (bi, j, 0, 0)),
        out_shape=jax.ShapeDtypeStruct((b, s // ATT_BLK, n, ATT_BLK), BF16),
        compiler_params=_cparams(("parallel", "parallel")),
        name="proj_t",
    )(x3d, wt_all)


def _split_heads(q2):
    lane = lax.broadcasted_iota(jnp.int32, q2.shape, 1)
    zero = jnp.zeros_like(q2)
    q_a = jnp.where(lane < HEAD_DIM, q2, zero)
    q_b = jnp.where(lane >= HEAD_DIM, q2, zero)
    return jnp.concatenate([q_a, q_b], axis=0)


def _head_selector():
    r = lax.broadcasted_iota(jnp.int32, (SUBLANES, LANES), 0)
    lane = lax.broadcasted_iota(jnp.int32, (SUBLANES, LANES), 1)
    return jnp.where(((r == 0) & (lane < HEAD_DIM)) | ((r == 1) & (lane >= HEAD_DIM)), 1.0, 0.0).astype(BF16)


def _sq_norms_t(sel, x):
    xf = x.astype(F32)
    return lax.dot_general(sel, (xf * xf).astype(BF16), _NT, preferred_element_type=F32)


def _store_key_norms(k_ref, knm_sc, n_blocks):
    sel = _head_selector()
    for g in range(ATT_GROUPS):
        kn2 = jnp.zeros((SUBLANES, 1), F32)
        for j in range(n_blocks):
            kj = k_ref[0, j * ATT_BLK:(j + 1) * ATT_BLK, g * LANES:(g + 1) * LANES]
            kn2 = jnp.maximum(kn2, jnp.max(_sq_norms_t(sel, kj), axis=1, keepdims=True))
        knm_sc[g] = jnp.broadcast_to(kn2, (SUBLANES, LANES))


def _attend(q_rows, k_ref, vt_ref, bias_sc, knm_sc, i, m_sc, l_sc, acc_sc):
    tk = ATT_CHUNK * ATT_BLK
    groups = range(len(q_rows))
    n_cols = 2 * ATT_BLK

    def scores(g, start, size):
        k_c = k_ref[0, pl.ds(start, size), g * LANES:(g + 1) * LANES]
        return lax.dot_general(k_c, q_rows[g], _NT, preferred_element_type=F32)

    def add_rows(g, s_t, blk0, n, neg_shift):
        if bias_sc is None:
            return s_t if neg_shift is None else s_t - neg_shift
        s3 = s_t.reshape(n, ATT_BLK, n_cols) + bias_sc[g, pl.ds(blk0, n)]
        return s3.reshape(n * ATT_BLK, n_cols)

    def vt_cols(g, blk0, n):
        return jnp.concatenate([vt_ref[0, blk0 + b, g * LANES:(g + 1) * LANES, :] for b in range(n)], axis=1)

    def causal(s_own):
        key_pos = lax.broadcasted_iota(jnp.int32, s_own.shape, 0)
        q_pos = lax.broadcasted_iota(jnp.int32, s_own.shape, 1) & (ATT_BLK - 1)
        return jnp.where(key_pos <= q_pos, s_own, NEG)

    def sublane_sums(p):
        return jnp.sum(p.reshape(-1, SUBLANES, n_cols), axis=0)

    ones_row = (lax.broadcasted_iota(jnp.int32, (SUBLANES, LANES), 0) == 0).astype(BF16)
    col = lax.broadcasted_iota(jnp.int32, (1, n_cols), 1)
    bound = []
    for g in groups:
        qn2 = _sq_norms_t(ones_row, q_rows[g])[0:1]
        kn2 = jnp.where(col < ATT_BLK, knm_sc[g, 0:1, 0:1], knm_sc[g, 1:2, 0:1])
        bound.append(jnp.sqrt(qn2 * kn2) * BOUND_SLACK)
    worst = functools.reduce(jnp.maximum, [jnp.max(b) for b in bound])
    fast = worst <= FAST_BOUND

    def fast_tail(n_tail):
        blk0 = i - n_tail
        start = pl.multiple_of(blk0 * ATT_BLK, ATT_BLK)
        for g in groups:
            s_t = scores(g, start, (n_tail + 1) * ATT_BLK)
            s_own = causal(s_t[n_tail * ATT_BLK:])
            s_t = jnp.concatenate([s_t[:n_tail * ATT_BLK], s_own], axis=0) if n_tail else s_own
            p = jnp.exp2(add_rows(g, s_t, blk0, n_tail + 1, bound[g]))
            l_sc[g] = sublane_sums(p)
            acc_sc[g] = jnp.dot(vt_cols(g, blk0, n_tail + 1), p.astype(BF16), preferred_element_type=F32)

    def fast_body(c, carry):
        start = pl.multiple_of(c * tk, tk)
        for g in groups:
            p = jnp.exp2(add_rows(g, scores(g, start, tk), c * ATT_CHUNK, ATT_CHUNK, bound[g]))
            l_sc[g] += sublane_sums(p)
            acc_sc[g] += jnp.dot(vt_cols(g, c * ATT_CHUNK, ATT_CHUNK), p.astype(BF16),
                                 preferred_element_type=F32)
        return carry

    @pl.when(fast)
    def _():
        if bias_sc is not None:
            for g in groups:
                bias_sc[g] = bias_sc[g] - bound[g]
        for n_tail in range(ATT_CHUNK):
            leftover = (i & (ATT_CHUNK - 1)) == n_tail
            pl.when(leftover & (i < ATT_CHUNK))(functools.partial(fast_tail, n_tail))
            pl.when(leftover & (i >= ATT_CHUNK))(functools.partial(fast_tail, n_tail + ATT_CHUNK))
        lax.fori_loop(0, jnp.maximum(i // ATT_CHUNK - 1, 0), fast_body, 0)
        for g in groups:
            acc_sc[g] = acc_sc[g] / jnp.sum(l_sc[g], axis=0, keepdims=True)

    @pl.when(jnp.logical_not(fast))
    def _():
        own = pl.multiple_of(i * ATT_BLK, ATT_BLK)
        for g in groups:
            s_t = causal(scores(g, own, ATT_BLK))
            m = jnp.max(s_t, axis=0, keepdims=True)
            p = jnp.exp2(s_t - m)
            m_sc[g] = m
            l_sc[g, 0:1, :] = jnp.sum(p, axis=0, keepdims=True)
            acc_sc[g] = jnp.dot(vt_cols(g, i, 1), p.astype(BF16), preferred_element_type=F32)

        def exact_body(j, carry):
            start = pl.multiple_of(j * ATT_BLK, ATT_BLK)
            for g in groups:
                s_j = add_rows(g, scores(g, start, ATT_BLK), j, 1, None)
                m_old = m_sc[g]
                m_new = jnp.maximum(m_old, jnp.max(s_j, axis=0, keepdims=True))
                alpha = jnp.exp2(m_old - m_new)
                p = jnp.exp2(s_j - m_new)
                l_sc[g, 0:1, :] = alpha * l_sc[g, 0:1, :] + jnp.sum(p, axis=0, keepdims=True)
                acc_sc[g] = alpha * acc_sc[g] + jnp.dot(vt_cols(g, j, 1), p.astype(BF16),
                                                        preferred_element_type=F32)
                m_sc[g] = m_new
            return carry

        lax.fori_loop(0, i, exact_body, 0)
        for g in groups:
            acc_sc[g] = acc_sc[g] / l_sc[g, 0:1, :]

    return [acc_sc[g] for g in groups]


def _attn_scratch():
    return [
        pltpu.VMEM((ATT_GROUPS, SUBLANES, LANES), F32),
        pltpu.VMEM((ATT_GROUPS, 1, 2 * ATT_BLK), F32),
        pltpu.VMEM((ATT_GROUPS, SUBLANES, 2 * ATT_BLK), F32),
        pltpu.VMEM((ATT_GROUPS, LANES, 2 * ATT_BLK), F32),
    ]


def _moba_kernel(q_ref, k_ref, vt_ref, o_ref, kmean_sc, bias_sc, knm_sc, m_sc, l_sc, acc_sc, *, n_blocks):
    i = pl.program_id(2)
    tq = ATT_BLK

    @pl.when(i == 0)
    def _():
        _store_key_norms(k_ref, knm_sc, n_blocks)
        for g in range(ATT_GROUPS):
            for j in range(n_blocks):
                kj = k_ref[0, j * MOBA_BLOCK:(j + 1) * MOBA_BLOCK, g * LANES:(g + 1) * LANES].astype(F32)
                kmean_sc[g, j:j + 1, :] = jnp.mean(kj, axis=0, keepdims=True)

    q_rows = []
    for g in range(ATT_GROUPS):
        qs = _split_heads(q_ref[0, :, g * LANES:(g + 1) * LANES])
        q_rows.append(qs)
        gate_t = lax.dot_general(kmean_sc[g].astype(BF16), qs, _NT, preferred_element_type=F32)
        blk = lax.broadcasted_iota(jnp.int32, gate_t.shape, 0)
        cnt = jnp.zeros(gate_t.shape, F32)
        for jp in range(n_blocks):
            row = gate_t[jp:jp + 1, :]
            beats = (row > gate_t) | ((row == gate_t) & (jp < blk))
            cnt = cnt + jnp.where(beats, jnp.where(jp < i, 1.0, 0.0), 0.0)
        selected = (cnt < MOBA_TOPK) & (blk < i)
        bias_t = jnp.where(selected | (blk == i), 0.0, NEG)
        for n in range(n_blocks):
            bias_sc[g, n] = bias_t[n:n + 1, :]

    o_t = _attend(q_rows, k_ref, vt_ref, bias_sc, knm_sc, i, m_sc, l_sc, acc_sc)
    for g in range(ATT_GROUPS):
        o_sel = jnp.concatenate([o_t[g][:HEAD_DIM, :tq], o_t[g][HEAD_DIM:, tq:]], axis=0)
        o_ref[0, :, g * LANES:(g + 1) * LANES] = o_sel.T.astype(o_ref.dtype)


def _moba_attention(qk, vt, d_model):
    b, s, _ = qk.shape
    gw = ATT_GROUPS * LANES
    n_steps = d_model // gw
    n_blocks = s // MOBA_BLOCK
    return pl.pallas_call(
        functools.partial(_moba_kernel, n_blocks=n_blocks),
        grid=(b, n_steps, n_blocks),
        in_specs=[
            pl.BlockSpec((1, ATT_BLK, gw), lambda bi, g, i: (bi, i, g)),
            pl.BlockSpec((1, s, gw), lambda bi, g, i: (bi, 0, n_steps + g)),
            pl.BlockSpec((1, n_blocks, gw, ATT_BLK), lambda bi, g, i: (bi, 0, g, 0)),
        ],
        out_specs=pl.BlockSpec((1, ATT_BLK, gw), lambda bi, g, i: (bi, i, g)),
        out_shape=jax.ShapeDtypeStruct((b, s, d_model), BF16),
        scratch_shapes=[pltpu.VMEM((ATT_GROUPS, n_blocks, LANES), F32),
                        pltpu.VMEM((ATT_GROUPS, n_blocks, 1, 2 * ATT_BLK), F32),
                        ] + _attn_scratch(),
        compiler_params=_cparams(("parallel", "parallel", "arbitrary")),
        name="moba_attn",
    )(qk, qk, vt)


def _diff_kernel(q_ref, k_ref, vt_ref, lam_ref, g_ref, o_ref, knm_sc, m_sc, l_sc, acc_sc, *, lambda_init, n_blocks):
    i = pl.program_id(2)
    tq = ATT_BLK

    @pl.when(i == 0)
    def _():
        _store_key_norms(k_ref, knm_sc, n_blocks)

    q_rows = [_split_heads(q_ref[0, :, g * LANES:(g + 1) * LANES])
              for g in range(ATT_GROUPS)]
    o_t = _attend(q_rows, k_ref, vt_ref, None, knm_sc, i, m_sc, l_sc, acc_sc)

    lp = lam_ref[...].astype(F32)
    lam = (jnp.exp(jnp.sum(lp[0:1] * lp[1:2], axis=1, keepdims=True))
           - jnp.exp(jnp.sum(lp[2:3] * lp[3:4], axis=1, keepdims=True)) + lambda_init)
    for g in range(ATT_GROUPS):
        o = (o_t[g][:, :tq] - lam * o_t[g][:, tq:]).T
        o = o * lax.rsqrt(jnp.mean(o * o, axis=1, keepdims=True) + SUBLN_EPS)
        o_ref[0, :, g * LANES:(g + 1) * LANES] = (o * g_ref[...] * (1.0 - lambda_init)).astype(o_ref.dtype)


def _diff_attention(q, k, vt, lam_params, subln_g, lambda_init, d_model):
    b, s, _ = q.shape
    gw = ATT_GROUPS * LANES
    n_steps = d_model // gw
    n_blocks = s // ATT_BLK
    return pl.pallas_call(
        functools.partial(_diff_kernel, lambda_init=lambda_init, n_blocks=n_blocks),
        grid=(b, n_steps, n_blocks),
        in_specs=[
            pl.BlockSpec((1, ATT_BLK, gw), lambda bi, h, i: (bi, i, h)),
            pl.BlockSpec((1, s, gw), lambda bi, h, i: (bi, 0, h)),
            pl.BlockSpec((1, n_blocks, gw, ATT_BLK), lambda bi, h, i: (bi, 0, h, 0)),
            pl.BlockSpec(lam_params.shape, lambda bi, h, i: (0, 0)),
            pl.BlockSpec((1, LANES), lambda bi, h, i: (0, 0)),
        ],
        out_specs=pl.BlockSpec((1, ATT_BLK, gw), lambda bi, h, i: (bi, i, h)),
        out_shape=jax.ShapeDtypeStruct((b, s, d_model), BF16),
        scratch_shapes=_attn_scratch(),
        compiler_params=_cparams(("parallel", "parallel", "arbitrary")),
        name="diff_attn",
    )(q, k, vt, lam_params, subln_g)


def _layer_norm(z, g, b):
    mu = jnp.mean(z, axis=1, keepdims=True)
    zc = z - mu
    var = jnp.mean(zc * zc, axis=1, keepdims=True)
    return zc * lax.rsqrt(var + LN_EPS) * g + b


def _oproj_ln_kernel(x_ref, a_ref, w_ref, g_ref, b_ref, o_ref, *, alpha):
    rows = x_ref.shape[0] // OPROJ_ROW_GROUPS
    for r0 in range(0, x_ref.shape[0], rows):
        y = jnp.dot(a_ref[r0:r0 + rows, :], w_ref[...], preferred_element_type=F32)
        o_ref[r0:r0 + rows, :] = _layer_norm(alpha * x_ref[r0:r0 + rows, :] + y, g_ref[...], b_ref[...])


def _oproj_ln(x2d, a2d, w_all, layer, g, b, alpha):
    t, d = x2d.shape
    tm = PROJ_TM
    return pl.pallas_call(
        functools.partial(_oproj_ln_kernel, alpha=alpha),
        grid=(t // tm,),
        in_specs=[
            pl.BlockSpec((tm, d), lambda i: (i, 0)),
            pl.BlockSpec((tm, d), lambda i: (i, 0)),
            pl.BlockSpec((None, d, d), lambda i: (layer, 0, 0)),
            pl.BlockSpec((1, d), lambda i: (0, 0)),
            pl.BlockSpec((1, d), lambda i: (0, 0)),
        ],
        out_specs=pl.BlockSpec((tm, d), lambda i: (i, 0)),
        out_shape=jax.ShapeDtypeStruct((t, d), F32),
        compiler_params=_cparams(("parallel",)),
        name="oproj_ln",
    )(x2d, a2d, w_all, g, b)


def _gelu_tanh(x):
    return 0.5 * x * (1.0 + jnp.tanh(math.sqrt(2.0 / math.pi) * (x + 0.044715 * (x * x * x))))


def _ffn_kernel(x_ref, halo_ref, wi_ref, cw_ref, cb_ref, wo_ref, g_ref, b_ref, o_ref, up_sc,
                *, alpha, tiles_per_seq, tf):
    i = pl.program_id(0)
    tm = x_ref.shape[0]
    d_ff = wo_ref.shape[0]
    x = x_ref[...]
    xb = x.astype(BF16)
    seq_start = (i % tiles_per_seq) == 0
    halo = jnp.where(seq_start, 0.0, halo_ref[...]).astype(BF16)
    x_ext = jnp.concatenate([halo, xb], axis=0)
    f = None
    for c0 in range(0, d_ff, tf):
        up_sc[...] = jnp.dot(x_ext, wi_ref[:, c0:c0 + tf], preferred_element_type=F32)
        gate = jnp.dot(xb, wi_ref[:, d_ff + c0:d_ff + c0 + tf], preferred_element_type=F32)
        cw = cw_ref[:, c0:c0 + tf]
        conv = (up_sc[FFN_HALO - 2:FFN_HALO - 2 + tm, :] * cw[0:1]
                + up_sc[FFN_HALO - 1:FFN_HALO - 1 + tm, :] * cw[1:2]
                + up_sc[FFN_HALO:FFN_HALO + tm, :] * cw[2:3]
                + cb_ref[:, c0:c0 + tf])
        h = (_gelu_tanh(conv) * gate).astype(BF16)
        part = jnp.dot(h, wo_ref[c0:c0 + tf, :], preferred_element_type=F32)
        f = part if f is None else f + part
    o_ref[...] = _layer_norm(alpha * x + f, g_ref[...], b_ref[...])


def _ffn_tile(d_ff):
    n = d_ff // LANES
    for parts in range(2, n + 1):
        if n % parts == 0:
            return d_ff // parts
    return d_ff


def _conv_ffn_ln(x2d, w_in_all, conv_w, conv_b, w_out_all, layer, g, b, alpha, seq):
    t, d = x2d.shape
    d_ff = w_out_all.shape[1]
    tm = FFN_TM
    tf = _ffn_tile(d_ff)
    halo_blocks = tm // FFN_HALO
    resident = pl.Buffered(1)
    return pl.pallas_call(
        functools.partial(_ffn_kernel, alpha=alpha, tiles_per_seq=seq // tm, tf=tf),
        grid=(t // tm,),
        in_specs=[
            pl.BlockSpec((tm, d), lambda i: (i, 0)),
            pl.BlockSpec((FFN_HALO, d), lambda i: (jnp.maximum(i * halo_blocks - 1, 0), 0)),
            pl.BlockSpec((None, d, 2 * d_ff), lambda i: (layer, 0, 0), pipeline_mode=resident),
            pl.BlockSpec((CONV_WIDTH, d_ff), lambda i: (0, 0)),
            pl.BlockSpec((1, d_ff), lambda i: (0, 0)),
            pl.BlockSpec((None, d_ff, d), lambda i: (layer, 0, 0), pipeline_mode=resident),
            pl.BlockSpec((1, d), lambda i: (0, 0)),
            pl.BlockSpec((1, d), lambda i: (0, 0)),
        ],
        out_specs=pl.BlockSpec((tm, d), lambda i: (i, 0)),
        out_shape=jax.ShapeDtypeStruct((t, d), F32),
        scratch_shapes=[pltpu.VMEM((FFN_HALO + tm, tf), F32)],
        compiler_params=_cparams(("parallel",)),
        name="conv_ffn_ln",
    )(x2d, x2d, w_in_all, conv_w, conv_b, w_out_all, g, b)


def _rotary_tables(positions):
    inv_freq = 1.0 / (ROPE_THETA ** (jnp.arange(ROPE_HALF, dtype=F32) * 2.0 / ROPE_ROT))
    ang = positions.astype(F32).reshape(-1, 1) * inv_freq
    cos, sin = jnp.cos(ang), jnp.sin(ang)
    t = ang.shape[0]
    ones = jnp.ones((t, HEAD_DIM - ROPE_ROT), F32)
    zeros = jnp.zeros((t, HEAD_DIM - ROPE_ROT), F32)
    c_head = jnp.concatenate([cos, cos, ones], axis=1)
    s_head = jnp.concatenate([-sin, sin, zeros], axis=1)
    reps = LANES // HEAD_DIM
    return jnp.tile(c_head, (1, reps)), jnp.tile(s_head, (1, reps))


def kernel(x, positions, a_w_qkv, a_w_o, w_kv_shared, b_w_q, b_w_o, b_lambda, b_subln_g, ln_g, ln_b,
           ffn_w_in, ffn_conv_w, ffn_conv_b, ffn_w_out):
    bsz, seq, d = x.shape
    depth = ffn_w_in.shape[0]
    n_a = a_w_qkv.shape[0]
    t = bsz * seq
    alpha = (2 * depth) ** 0.25
    assert d % (ATT_GROUPS * LANES) == 0
    assert seq % max(PROJ_TM, FFN_TM, ATT_CHUNK * ATT_BLK, PROJ_T_BLOCKS * ATT_BLK) == 0

    cos_tab, sin_tab = _rotary_tables(positions)
    a_w_qkv_b, a_w_o_b = a_w_qkv.astype(BF16), a_w_o.astype(BF16)
    a_wv_t = jnp.swapaxes(a_w_qkv[:, :, 2 * d:], 1, 2).astype(BF16)
    w_kv_b = w_kv_shared.astype(BF16)[None]
    w_v_sh_t = w_kv_shared[:, d:].T.astype(BF16)[None]
    b_w_q_b, b_w_o_b = b_w_q.astype(BF16), b_w_o.astype(BF16)
    ffn_w_in_b, ffn_w_out_b = ffn_w_in.astype(BF16), ffn_w_out.astype(BF16)

    xs = x.reshape(t, d)
    k_sh = vt_sh = None
    for layer in range(depth):
        g0, b0 = ln_g[layer, 0].reshape(1, d), ln_b[layer, 0].reshape(1, d)
        g1, b1 = ln_g[layer, 1].reshape(1, d), ln_b[layer, 1].reshape(1, d)
        if layer < n_a:
            qk = _proj_rot(xs, a_w_qkv_b, layer, 2 * d, cos_tab, sin_tab, n_scaled=d)
            vt = _proj_t(xs.reshape(bsz, seq, d), a_wv_t, layer)
            o = _moba_attention(qk.reshape(bsz, seq, 2 * d), vt, d)
            xs = _oproj_ln(xs, o.reshape(t, d), a_w_o_b, layer, g0, b0, alpha)
        else:
            jb = layer - n_a
            if jb == 0:
                k_sh = _proj_rot(xs, w_kv_b, 0, d, cos_tab, sin_tab, n_scaled=0).reshape(bsz, seq, d)
                vt_sh = _proj_t(xs.reshape(bsz, seq, d), w_v_sh_t, 0)
            lambda_init = 0.8 - 0.6 * math.exp(-0.3 * layer)
            q = _proj_rot(xs, b_w_q_b, jb, d, cos_tab, sin_tab, n_scaled=d)
            o = _diff_attention(q.reshape(bsz, seq, d), k_sh, vt_sh, b_lambda[jb],
                                b_subln_g[jb].reshape(1, LANES), lambda_init, d)
            xs = _oproj_ln(xs, o.reshape(t, d), b_w_o_b, jb, g0, b0, alpha)
        xs = _conv_ffn_ln(xs, ffn_w_in_b, ffn_conv_w[layer], ffn_conv_b[layer].reshape(1, -1),
                          ffn_w_out_b, layer, g1, b1, alpha, seq)
    return xs.reshape(bsz, seq, d)
```
